```python
import math
import jax, jax.numpy as jnp
from jax import lax
import numpy as np

D_MODEL = 1024
BATCH = 8
SEQ = 2048
DEPTH = 4

BRANCH_WIDTH = 512
N_BRANCH = 3
EPS = 1e-6
NEG_BIG = -1e30
HG_HEADS = 4
HG_DIM = 128
HG_CHUNK = 64
AT_HEADS = 4
AT_DIM = 128
IDX_HEADS = 8
IDX_DIM = 64
TOPK_MAX = 256
Q_BLOCK = 128
ROPE_THETA = 10000.0
S5_GROUP = 16
S5_GROUPS = BRANCH_WIDTH // S5_GROUP
S5_STATE = 64

IN_SPLITS = (
    HG_HEADS * HG_DIM, HG_HEADS * HG_DIM, HG_HEADS * HG_DIM, BRANCH_WIDTH,
    AT_HEADS * AT_DIM, AT_DIM, AT_DIM, BRANCH_WIDTH,
    IDX_HEADS * IDX_DIM, IDX_DIM, IDX_HEADS,
    BRANCH_WIDTH, BRANCH_WIDTH,
    N_BRANCH * D_MODEL,
)
N_IN = 4 * BRANCH_WIDTH + (2 * BRANCH_WIDTH + 2 * AT_DIM) + (IDX_HEADS * IDX_DIM + IDX_DIM + IDX_HEADS) + 2 * BRANCH_WIDTH + N_BRANCH * D_MODEL

kernel_name = "hybrid_hgrn2_dsa_s5_gated_trunk"


def rms_norm(x, g):
    x32 = x.astype(jnp.float32)
    r = lax.rsqrt(jnp.mean(x32 * x32, axis=-1, keepdims=True) + EPS)
    return (x32 * r).astype(x.dtype) * g


def rope_tables(positions, dim):
    inv_freq = ROPE_THETA ** (-jnp.arange(0, dim, 2, dtype=jnp.float32) / dim)
    ang = positions.astype(jnp.float32)[..., None] * inv_freq
    return jnp.cos(ang)[:, :, None, :], jnp.sin(ang)[:, :, None, :]


def apply_rope(x, cos, sin):
    half = x.shape[-1] // 2
    x1 = x[..., :half].astype(jnp.float32)
    x2 = x[..., half:].astype(jnp.float32)
    out = jnp.concatenate([x1 * cos - x2 * sin, x2 * cos + x1 * sin], axis=-1)
    return out.astype(x.dtype)


def split_cols(p):
    out = []
    o = 0
    for n in IN_SPLITS:
        out.append(p[..., o:o + n])
        o += n
    return out


def hgrn2_mixer(q, fz, i, lb, onorm_g):
    B, S, W = q.shape
    C = HG_CHUNK
    NC = S // C
    lb = lb.astype(jnp.float32)
    sg = jax.nn.sigmoid(fz.astype(jnp.float32))
    f = lb + (1.0 - lb) * sg
    log_f = jnp.log(jnp.maximum(f, 1e-30))
    k = 1.0 - f

    def to_chunks(a):
        return a.astype(jnp.float32).reshape(B, NC, C, HG_HEADS, HG_DIM).transpose(1, 0, 3, 2, 4)

    tri = jnp.tril(jnp.ones((C, C), dtype=bool))[None, None, :, :, None]

    def step(state, inp):
        qc, kc, vc, lfc = inp
        b = jnp.cumsum(lfc, axis=2)
        o_inter = jnp.einsum('bhtd,bhde->bhte', qc * jnp.exp(b), state)
        diff = b[:, :, :, None, :] - b[:, :, None, :, :]
        decay = jnp.where(tri, jnp.exp(jnp.where(tri, diff, 0.0)), 0.0)
        scores = jnp.einsum('bhtd,bhsd,bhtsd->bhts', qc, kc, decay)
        o_intra = jnp.einsum('bhts,bhse->bhte', scores, vc)
        b_last = b[:, :, -1:, :]
        k_dec = kc * jnp.exp(b_last - b)
        state = jnp.exp(b_last[:, :, 0, :])[..., None] * state + jnp.einsum('bhsd,bhse->bhde', k_dec, vc)
        return state, o_inter + o_intra

    state0 = jnp.zeros((B, HG_HEADS, HG_DIM, HG_DIM), jnp.float32)
    _, o = lax.scan(step, state0, (to_chunks(q), to_chunks(k), to_chunks(i), to_chunks(log_f)))
    o = o.transpose(1, 0, 3, 2, 4).reshape(B, S, HG_HEADS, HG_DIM).astype(q.dtype)
    o = rms_norm(o, onorm_g)
    return o.reshape(B, S, W)


def dsa_mixer(q, k, v, iq, ik, iw, positions, qn_g, kn_g):
    B, S, _ = q.shape
    topk = min(TOPK_MAX, S // 4)
    NB = S // Q_BLOCK
    cos, sin = rope_tables(positions, AT_DIM)
    cos_i, sin_i = rope_tables(positions, IDX_DIM)
    q = apply_rope(rms_norm(q.reshape(B, S, AT_HEADS, AT_DIM), qn_g), cos, sin)
    k = apply_rope(rms_norm(k[:, :, None, :], kn_g), cos, sin)[:, :, 0, :]
    iq = apply_rope(iq.reshape(B, S, IDX_HEADS, IDX_DIM), cos_i, sin_i)
    ik = apply_rope(ik[:, :, None, :], cos_i, sin_i)[:, :, 0, :]
    scale = 1.0 / math.sqrt(AT_DIM)
    key_pos = jnp.arange(S)

    def blocks(a):
        return jnp.moveaxis(a.reshape((B, NB, Q_BLOCK) + a.shape[2:]), 1, 0)

    def one_block(args):
        qb, iqb, iwb, blk = args
        qpos = blk * Q_BLOCK + jnp.arange(Q_BLOCK)
        rel = jax.nn.relu(jnp.einsum('bqhd,bsd->bqhs', iqb, ik).astype(jnp.float32))
        isc = jnp.einsum('bqhs,bqh->bqs', rel, iwb.astype(jnp.float32))
        causal = key_pos[None, :] <= qpos[:, None]
        isc = jnp.where(causal[None], isc, NEG_BIG)
        _, idx = lax.top_k(isc, topk)
        ksel = jax.vmap(lambda kk, ii: kk[ii])(k, idx)
        vsel = jax.vmap(lambda vv, ii: vv[ii])(v, idx)
        logits = jnp.einsum('bqhd,bqkd->bqhk', qb, ksel).astype(jnp.float32) * scale
        valid = idx <= qpos[None, :, None]
        logits = jnp.where(valid[:, :, None, :], logits, NEG_BIG)
        p = jax.nn.softmax(logits, axis=-1).astype(v.dtype)
        return jnp.einsum('bqhk,bqkd->bqhd', p, vsel)

    out = lax.map(one_block, (blocks(q), blocks(iq), blocks(iw), jnp.arange(NB)))
    return jnp.moveaxis(out, 0, 1).reshape(B, S, AT_HEADS * AT_DIM)


def s5_mixer(u, a_re, a_im, log_dt, b_re, b_im, c_re, c_im, d_skip, glu_w, glu_b):
    B, S, W = u.shape
    f32 = jnp.float32
    ug = u.astype(f32).reshape(B, S, S5_GROUPS, S5_GROUP)
    a_re = a_re.astype(f32); a_im = a_im.astype(f32)
    dt = jnp.exp(log_dt.astype(f32))[:, None]
    mag = jnp.exp(a_re * dt)
    ang = a_im * dt
    abar_re = mag * jnp.cos(ang)
    abar_im = mag * jnp.sin(ang)
    nr = abar_re - 1.0
    ni = abar_im
    den = a_re * a_re + a_im * a_im
    fr = (nr * a_re + ni * a_im) / den
    fi = (ni * a_re - nr * a_im) / den
    b_re = b_re.astype(f32); b_im = b_im.astype(f32)
    bbar_re = fr[..., None] * b_re - fi[..., None] * b_im
    bbar_im = fr[..., None] * b_im + fi[..., None] * b_re
    bu_re = jnp.einsum('bsgc,gpc->bsgp', ug, bbar_re)
    bu_im = jnp.einsum('bsgc,gpc->bsgp', ug, bbar_im)
    ar_t = jnp.broadcast_to(abar_re, bu_re.shape)
    ai_t = jnp.broadcast_to(abar_im, bu_re.shape)

    def combine(e1, e2):
        a1r, a1i, b1r, b1i = e1
        a2r, a2i, b2r, b2i = e2
        return (a2r * a1r - a2i * a1i,
                a2r * a1i + a2i * a1r,
                a2r * b1r - a2i * b1i + b2r,
                a2r * b1i + a2i * b1r + b2i)

    _, _, xr, xi = lax.associative_scan(combine, (ar_t, ai_t, bu_re, bu_im), axis=1)
    y = (jnp.einsum('gcp,bsgp->bsgc', c_re.astype(f32), xr)
         - jnp.einsum('gcp,bsgp->bsgc', c_im.astype(f32), xi))
    y = y.reshape(B, S, W) + d_skip.astype(f32) * u.astype(f32)
    y = jax.nn.gelu(y).astype(u.dtype)
    return y * jax.nn.sigmoid(y @ glu_w + glu_b)


def hybrid_layer(x, c, lb, ada_w, ada_b, norm_g, w_in, hg_onorm_g, at_qnorm_g, at_knorm_g,
                 s5_a_re, s5_a_im, s5_log_dt, s5_b_re, s5_b_im, s5_c_re, s5_c_im, s5_d,
                 s5_glu_w, s5_glu_b, w_branch, w_out, positions):
    B, S, D = x.shape
    mod = jax.nn.silu(c) @ ada_w + ada_b
    shift, scale, gate = jnp.split(mod[:, None, :], 3, axis=-1)
    h = rms_norm(x, norm_g) * (1.0 + scale) + shift
    proj = h @ w_in
    (hg_q, hg_f, hg_i, hg_g, at_q, at_k, at_v, at_g,
     ix_q, ix_k, ix_w, s5_u, s5_g, merge_g) = split_cols(proj)
    ya = hgrn2_mixer(hg_q, hg_f, hg_i, lb, hg_onorm_g) * jax.nn.silu(hg_g)
    yb = dsa_mixer(at_q, at_k, at_v, ix_q, ix_k, ix_w, positions, at_qnorm_g, at_knorm_g) * jax.nn.silu(at_g)
    yc = s5_mixer(s5_u, s5_a_re, s5_a_im, s5_log_dt, s5_b_re, s5_b_im, s5_c_re, s5_c_im,
                  s5_d, s5_glu_w, s5_glu_b) * jax.nn.silu(s5_g)
    ys = jnp.stack([ya, yb, yc], axis=2)
    yd = jnp.einsum('bsnw,nwd->bsnd', ys, w_branch)
    gates = jax.nn.sigmoid(merge_g).reshape(B, S, N_BRANCH, D)
    merged = jnp.einsum('bsnd,bsnd->bsd', gates, yd)
    return x + gate * (merged @ w_out)


def setup_inputs(seed: int = 0) -> dict:
    key = jax.random.key(seed)
    ks = jax.random.split(key, 26)
    f32 = jnp.float32
    L, D, W, G, P = DEPTH, D_MODEL, BRANCH_WIDTH, S5_GROUPS, S5_STATE
    nrm = lambda k, shape, s: jax.random.normal(k, shape, f32) * s
    x = nrm(ks[0], (BATCH, SEQ, D), 1.0)
    c = nrm(ks[1], (BATCH, D), 1.0)
    offsets = jax.random.randint(ks[2], (BATCH, 1), 0, 1024, dtype=jnp.int32)
    positions = (offsets + jnp.arange(SEQ, dtype=jnp.int32)[None, :]).astype(jnp.int32)
    n_idx = jnp.arange(P, dtype=f32)
    return {
        "x": x,
        "c": c,
        "positions": positions,
        "ada_w": nrm(ks[3], (L, D, 3 * D), 0.5 * D ** -0.5),
        "ada_b": nrm(ks[4], (L, 3 * D), 0.02),
        "norm_g": 1.0 + nrm(ks[5], (L, D), 0.01),
        "w_in": nrm(ks[6], (L, D, N_IN), D ** -0.5),
        "hg_lb_logits": nrm(ks[7], (L, HG_HEADS * HG_DIM), 0.1),
        "hg_onorm_g": 1.0 + nrm(ks[8], (L, HG_DIM), 0.01),
        "at_qnorm_g": 1.0 + nrm(ks[9], (L, AT_DIM), 0.01),
        "at_knorm_g": 1.0 + nrm(ks[10], (L, AT_DIM), 0.01),
        "s5_a_re": -0.5 + nrm(ks[11], (L, G, P), 0.01),
        "s5_a_im": math.pi * n_idx[None, None, :] + nrm(ks[12], (L, G, P), 0.01),
        "s5_log_dt": jax.random.uniform(ks[13], (L, G), f32, math.log(1e-3), math.log(1e-1)),
        "s5_b_re": nrm(ks[14], (L, G, P, S5_GROUP), (2.0 * S5_GROUP) ** -0.5),
        "s5_b_im": nrm(ks[15], (L, G, P, S5_GROUP), (2.0 * S5_GROUP) ** -0.5),
        "s5_c_re": nrm(ks[16], (L, G, S5_GROUP, P), (2.0 * P) ** -0.5),
        "s5_c_im": nrm(ks[17], (L, G, S5_GROUP, P), (2.0 * P) ** -0.5),
        "s5_d": nrm(ks[18], (L, W), 1.0),
        "s5_glu_w": nrm(ks[19], (L, W, W), W ** -0.5),
        "s5_glu_b": nrm(ks[20], (L, W), 0.02),
        "w_branch": nrm(ks[21], (L, N_BRANCH, W, D), W ** -0.5),
        "w_out": nrm(ks[22], (L, D, D), D ** -0.5),
    }


def reference(x, c, positions, ada_w, ada_b, norm_g, w_in, hg_lb_logits, hg_onorm_g,
              at_qnorm_g, at_knorm_g, s5_a_re, s5_a_im, s5_log_dt, s5_b_re, s5_b_im,
              s5_c_re, s5_c_im, s5_d, s5_glu_w, s5_glu_b, w_branch, w_out):
    p = jax.nn.softmax(hg_lb_logits.astype(jnp.float32), axis=0)
    lb_all = jnp.cumsum(p, axis=0) - p[0:1]
    for l in range(DEPTH):
        x = hybrid_layer(x, c, lb_all[l], ada_w[l], ada_b[l], norm_g[l], w_in[l], hg_onorm_g[l],
                         at_qnorm_g[l], at_knorm_g[l], s5_a_re[l], s5_a_im[l], s5_log_dt[l],
                         s5_b_re[l], s5_b_im[l], s5_c_re[l], s5_c_im[l], s5_d[l],
                         s5_glu_w[l], s5_glu_b[l], w_branch[l], w_out[l], positions)
    return x
```

```python
import functools
import math

import jax
import jax.numpy as jnp
from jax import lax
from jax.experimental import pallas as pl
from jax.experimental.pallas import tpu as pltpu

F32 = jnp.float32
BF16 = jnp.bfloat16

D_MODEL = 1024
DEPTH = 4
BRANCH_WIDTH = 512
N_BRANCH = 3
EPS = 1e-6
NEG_BIG = -1e30
HG_HEADS = 4
HG_DIM = 128
AT_HEADS = 4
AT_DIM = 128
IDX_HEADS = 8
IDX_DIM = 64
TOPK_MAX = 256
Q_BLOCK = 128
ROPE_THETA = 10000.0
S5_GROUP = 16
S5_GROUPS = BRANCH_WIDTH // S5_GROUP
S5_STATE = 64

LANES = 128
HG_CHUNK = 128
HG_SUB = 16
S5_BLOCK = 16
VMEM_LIMIT = 52 * 1024 * 1024

COL_MERGE = 0
COL_HG_Q = 3072
COL_HG_F = 3584
COL_HG_I = 4096
COL_HG_G = 4608
COL_AT_Q = 5120
COL_AT_G = 5632
COL_S5_U = 6144
COL_S5_G = 6656
COL_IX_Q = 7168
COL_AT_K = 7680
COL_AT_V = 7808
COL_IX_KW = 7936
N_PROJ = 8064
PROJ_TN = 1152

_NT = (((1,), (1,)), ((), ()))
_TN = (((0,), (0,)), ((), ()))


def _cparams(*sem):
    return pltpu.CompilerParams(dimension_semantics=sem, vmem_limit_bytes=VMEM_LIMIT)


def _sigmoid(x):
    return 1.0 / (1.0 + jnp.exp(-x))


def _silu(x):
    return x * _sigmoid(x)


def _lb_kernel(z_ref, o_ref):
    z = z_ref[...]
    e = jnp.exp(z - jnp.max(z, axis=0, keepdims=True))
    p = e / jnp.sum(e, axis=0, keepdims=True)
    acc = jnp.zeros_like(p[0:1])
    for l in range(z.shape[0]):
        acc = acc + p[l:l + 1]
        o_ref[l:l + 1, :] = acc - p[0:1]


def _lower_bounds(logits):
    return pl.pallas_call(
        _lb_kernel, out_shape=jax.ShapeDtypeStruct(logits.shape, F32), name="hg_lower_bounds",
    )(logits)


def _mod_kernel(c_ref, w_ref, b_ref, o_ref):
    c = c_ref[...]
    o_ref[0] = jnp.dot(_silu(c), w_ref[0], precision=lax.Precision.HIGHEST,
                       preferred_element_type=F32) + b_ref[0]


def _modulation(c, ada_w, ada_b):
    L, D, N = ada_w.shape
    B = c.shape[0]
    tn = 512
    return pl.pallas_call(
        _mod_kernel,
        grid=(L, N // tn),
        in_specs=[pl.BlockSpec((B, D), lambda l, n: (0, 0)),
                  pl.BlockSpec((1, D, tn), lambda l, n: (l, 0, n)),
                  pl.BlockSpec((1, 1, tn), lambda l, n: (l, 0, n))],
        out_specs=pl.BlockSpec((1, B, tn), lambda l, n: (l, 0, n)),
        out_shape=jax.ShapeDtypeStruct((L, B, N), F32),
        compiler_params=_cparams("parallel", "parallel"),
        name="adaln_modulation",
    )(c, ada_w, ada_b.reshape(L, 1, N))


def _proj_kernel(x_ref, g_ref, shift_ref, scale_ref, w_ref, o_ref, h_ref):
    @pl.when(pl.program_id(1) == 0)
    def _():
        x = x_ref[...]
        r = lax.rsqrt(jnp.mean(x * x, axis=-1, keepdims=True) + EPS)
        h = (x * r) * g_ref[...] * (1.0 + scale_ref[0]) + shift_ref[0]
        h_ref[...] = h.astype(BF16)

    o_ref[...] = jnp.dot(h_ref[...], w_ref[...], preferred_element_type=F32)


def _projection(x2, norm_g, shift, scale, w_perm, seq):
    M, D = x2.shape
    tm = 512
    per_batch = seq // tm
    return pl.pallas_call(
        _proj_kernel,
        grid=(M // tm, N_PROJ // PROJ_TN),
        in_specs=[pl.BlockSpec((tm, D), lambda i, j: (i, 0)),
                  pl.BlockSpec((1, D), lambda i, j: (0, 0)),
                  pl.BlockSpec((1, 1, D), lambda i, j: (i // per_batch, 0, 0)),
                  pl.BlockSpec((1, 1, D), lambda i, j: (i // per_batch, 0, 0)),
                  pl.BlockSpec((D, PROJ_TN), lambda i, j: (0, j))],
        out_specs=pl.BlockSpec((tm, PROJ_TN), lambda i, j: (i, j)),
        out_shape=jax.ShapeDtypeStruct((M, N_PROJ), F32),
        scratch_shapes=[pltpu.VMEM((tm, D), BF16)],
        compiler_params=_cparams("parallel", "arbitrary"),
        name="norm_in_proj",
    )(x2, norm_g, shift, scale, w_perm)


def _permute_w_in(w):
    o = 0
    parts = {}
    for name, n in (("hg_q", 512), ("hg_f", 512), ("hg_i", 512), ("hg_g", 512),
                    ("at_q", 512), ("at_k", 128), ("at_v", 128), ("at_g", 512),
                    ("ix_q", 512), ("ix_k", 64), ("ix_w", 8),
                    ("s5_u", 512), ("s5_g", 512), ("merge", 3072)):
        parts[name] = w[:, o:o + n]
        o += n
    pad = jnp.zeros((w.shape[0], LANES - IDX_DIM - IDX_HEADS), w.dtype)
    cols = [parts["merge"], parts["hg_q"], parts["hg_f"], parts["hg_i"], parts["hg_g"],
            parts["at_q"], parts["at_g"], parts["s5_u"], parts["s5_g"], parts["ix_q"],
            parts["at_k"], parts["at_v"], parts["ix_k"], parts["ix_w"], pad]
    return jnp.concatenate(cols, axis=1).astype(BF16)


def _split3(x):
    hi = x.astype(BF16)
    r1 = x - hi.astype(F32)
    mid = r1.astype(BF16)
    lo = (r1 - mid.astype(F32)).astype(BF16)
    return hi, mid, lo


def _hgrn_kernel(q_ref, f_ref, i_ref, g_ref, lb_ref, on_ref, o_ref, state_ref, *, rows):
    C, c = HG_CHUNK, HG_SUB
    nsub = C // c

    @pl.when(pl.program_id(2) == 0)
    def _():
        state_ref[...] = jnp.zeros_like(state_ref)

    lb = lb_ref[...]
    row_id = lax.broadcasted_iota(jnp.int32, (C, C), 0)
    col_id = lax.broadcasted_iota(jnp.int32, (C, C), 1)
    tril = jnp.where(col_id <= row_id, 1.0, 0.0).astype(BF16)
    sub_row = lax.broadcasted_iota(jnp.int32, (c, C), 0)
    sub_col = lax.broadcasted_iota(jnp.int32, (c, C), 1)

    for ci in range(rows // C):
        sl = slice(ci * C, (ci + 1) * C)
        q = q_ref[sl, :]
        v = i_ref[sl, :]
        fg = lb + (1.0 - lb) * _sigmoid(f_ref[sl, :])
        lf = jnp.log(jnp.maximum(fg, 1e-30))
        k = 1.0 - fg
        hi, mid, lo = _split3(lf)
        b = (jnp.dot(tril, hi, preferred_element_type=F32)
             + jnp.dot(tril, mid, preferred_element_type=F32)
             + jnp.dot(tril, lo, preferred_element_type=F32))
        b_last = b[C - 1:C, :]

        state_t = state_ref[...]
        q0 = (q * jnp.exp(b)).astype(BF16)
        o = lax.dot_general(q0, state_t.astype(BF16), _NT, preferred_element_type=F32)

        v16 = v.astype(BF16)
        row_blocks = []
        for t in range(nsub):
            r0 = t * c
            bt = b[r0:r0 + c, :]
            qt = q[r0:r0 + c, :]
            diag = jnp.zeros((c, C), F32)
            for s in range(c):
                e = jnp.exp(jnp.minimum(bt - b[r0 + s:r0 + s + 1, :], 0.0))
                col = jnp.sum(qt * e * k[r0 + s:r0 + s + 1, :], axis=-1, keepdims=True)
                diag = jnp.where(sub_col == r0 + s, col, diag)
            diag = jnp.where(sub_col - r0 <= sub_row, diag, 0.0)
            if t == 0:
                row_blocks.append(diag)
                continue
            r_t = b[r0 - 1:r0, :]
            q_t = (qt * jnp.exp(bt - r_t)).astype(BF16)
            k_t = (k * jnp.exp(jnp.minimum(r_t - b, 0.0))).astype(BF16)
            off = lax.dot_general(q_t, k_t, _NT, preferred_element_type=F32)
            row_blocks.append(jnp.where(sub_col < r0, off, diag))
        scores = jnp.concatenate(row_blocks, axis=0)
        o = o + jnp.dot(scores.astype(BF16), v16, preferred_element_type=F32)

        k_dec = (k * jnp.exp(b_last - b)).astype(BF16)
        state_ref[...] = (state_t * jnp.exp(b_last)
                          + lax.dot_general(v16, k_dec, _TN, preferred_element_type=F32))

        r = lax.rsqrt(jnp.mean(o * o, axis=-1, keepdims=True) + EPS)
        o_ref[sl, :] = (o * r) * on_ref[...] * _silu(g_ref[sl, :])


def _hgrn(proj, lb, onorm_g, batch, seq):
    rows = min(512, seq)
    nr = seq // rows
    cb = lambda base: (lambda b, h, r: (b * nr + r, base // HG_DIM + h))
    return pl.pallas_call(
        functools.partial(_hgrn_kernel, rows=rows),
        grid=(batch, HG_HEADS, nr),
        in_specs=[pl.BlockSpec((rows, HG_DIM), cb(COL_HG_Q)),
                  pl.BlockSpec((rows, HG_DIM), cb(COL_HG_F)),
                  pl.BlockSpec((rows, HG_DIM), cb(COL_HG_I)),
                  pl.BlockSpec((rows, HG_DIM), cb(COL_HG_G)),
                  pl.BlockSpec((1, HG_DIM), lambda b, h, r: (0, h)),
                  pl.BlockSpec((1, HG_DIM), lambda b, h, r: (0, 0))],
        out_specs=pl.BlockSpec((rows, HG_DIM), lambda b, h, r: (b * nr + r, h)),
        out_shape=jax.ShapeDtypeStruct((batch * seq, BRANCH_WIDTH), F32),
        scratch_shapes=[pltpu.VMEM((HG_DIM, HG_DIM), F32)],
        compiler_params=_cparams("parallel", "parallel", "arbitrary"),
        name="hgrn2_mixer",
    )(proj, proj, proj, proj, lb, onorm_g)


def _rope_tables(positions):
    pos = positions.astype(F32)[..., None]

    def tables(dim, reps):
        inv = ROPE_THETA ** (-jnp.arange(0, dim, 2, dtype=F32) / dim)
        ang = pos * inv
        c, s = jnp.cos(ang), jnp.sin(ang)
        return (jnp.concatenate([c, c] * reps, axis=-1),
                jnp.concatenate([-s, s] * reps, axis=-1))

    return tables(AT_DIM, 1) + tables(IDX_DIM, LANES // IDX_DIM)


def _rope_full(x, cos, sin_signed):
    return x * cos + pltpu.roll(x, AT_DIM // 2, 1) * sin_signed


def _rope_idx(x, cos, sin_signed, first_half):
    h = IDX_DIM // 2
    partner = jnp.where(first_half, pltpu.roll(x, LANES - h, 1), pltpu.roll(x, h, 1))
    return x * cos + partner * sin_signed


def _dsa_kernel(q_ref, g_ref, iq_ref, k_ref, v_ref, ikw_ref, ca_ref, sa_ref, ci_ref, si_ref,
                qn_ref, kn_ref, o_ref, k_s, v_s, ik_s, key_s, val_s, *, seq, topk):
    j = pl.program_id(1)
    QB = Q_BLOCK
    lane = lax.broadcasted_iota(jnp.int32, (1, LANES), 1)
    first_half = (lane % IDX_DIM) < (IDX_DIM // 2)
    low_group = lane < IDX_DIM

    @pl.when(j == 0)
    def _():
        kk = k_ref[...]
        r = lax.rsqrt(jnp.mean(kk * kk, axis=-1, keepdims=True) + EPS)
        kk = (kk * r) * kn_ref[...]
        k_s[...] = _rope_full(kk, ca_ref[0], sa_ref[0]).astype(BF16)
        v_s[...] = v_ref[...].astype(BF16)
        ik = _rope_idx(ikw_ref[...], ci_ref[0], si_ref[0], first_half)
        ik_s[...] = jnp.where(low_group, ik, pltpu.roll(ik, IDX_DIM, 1)).astype(BF16)

    rows = pl.ds(pl.multiple_of(j * QB, QB), QB)
    ca, sa = ca_ref[0, rows, :], sa_ref[0, rows, :]
    ci, si = ci_ref[0, rows, :], si_ref[0, rows, :]
    ikw_q = ikw_ref[rows, :]

    isc = jnp.zeros((QB, seq), F32)
    for m in range(IDX_HEADS * IDX_DIM // LANES):
        xr = _rope_idx(iq_ref[:, m * LANES:(m + 1) * LANES], ci, si, first_half)
        for par in range(LANES // IDX_DIM):
            h = m * (LANES // IDX_DIM) + par
            xm = jnp.where(low_group if par == 0 else jnp.logical_not(low_group), xr, 0.0)
            rel = lax.dot_general(xm.astype(BF16), ik_s[...], _NT, preferred_element_type=F32)
            w = ikw_q[:, IDX_DIM + h:IDX_DIM + h + 1]
            isc = isc + jnp.maximum(rel, 0.0) * w
    qpos = j * QB + lax.broadcasted_iota(jnp.int32, (QB, 1), 0)
    kpos = lax.broadcasted_iota(jnp.int32, (1, seq), 1)
    causal = kpos <= qpos
    isc = jnp.where(causal, isc, NEG_BIG)
    bits = pltpu.bitcast(isc, jnp.int32)
    key_s[...] = jnp.where(bits < 0, bits ^ jnp.int32(0x7FFFFFFF), bits)

    kf = float(topk)

    def count_ge(cand):
        return jnp.sum(jnp.where(key_s[...] >= cand, 1.0, 0.0), axis=-1, keepdims=True)

    zero = jnp.zeros((QB, 1), jnp.int32)
    ans0 = jnp.where(count_ge(zero) >= kf, zero, jnp.full((QB, 1), -2**31, jnp.int32))

    def bisect(i, ans):
        cand = ans + (jnp.int32(1) << (30 - i))
        return jnp.where(count_ge(cand) >= kf, cand, ans)

    thr = lax.fori_loop(0, 31, bisect, ans0)

    key = key_s[...]
    gt = key > thr
    eq = key == thr
    need = kf - jnp.sum(jnp.where(gt, 1.0, 0.0), axis=-1, keepdims=True)
    rr = lax.broadcasted_iota(jnp.int32, (LANES, LANES), 0)
    cc = lax.broadcasted_iota(jnp.int32, (LANES, LANES), 1)
    upper = jnp.where(rr <= cc, 1.0, 0.0).astype(BF16)
    seen = jnp.zeros((QB, 1), F32)
    for cidx in range(seq // LANES):
        sl = slice(cidx * LANES, (cidx + 1) * LANES)
        eq_c = jnp.where(eq[:, sl], 1.0, 0.0)
        rank = jnp.dot(eq_c.astype(BF16), upper, preferred_element_type=F32) + seen
        take = jnp.where(gt[:, sl], 1.0, jnp.where(rank <= need, eq_c, 0.0))
        val_s[:, sl] = jnp.where(causal[:, sl], take, 0.0)
        seen = rank[:, LANES - 1:LANES]

    scale = 1.0 / math.sqrt(AT_DIM)
    valid = val_s[...] > 0.0
    for h in range(AT_HEADS):
        hs = slice(h * AT_DIM, (h + 1) * AT_DIM)
        qh = q_ref[:, hs]
        r = lax.rsqrt(jnp.mean(qh * qh, axis=-1, keepdims=True) + EPS)
        qh = _rope_full((qh * r) * qn_ref[...], ca, sa) * scale
        logits = lax.dot_general(qh.astype(BF16), k_s[...], _NT, preferred_element_type=F32)
        logits = jnp.where(valid, logits, NEG_BIG)
        p = jnp.exp(logits - jnp.max(logits, axis=-1, keepdims=True))
        denom = jnp.sum(p, axis=-1, keepdims=True)
        oh = jnp.dot(p.astype(BF16), v_s[...], preferred_element_type=F32) / denom
        o_ref[:, hs] = oh * _silu(g_ref[:, hs])


def _dsa(proj, tables, qn_g, kn_g, batch, seq):
    nb = seq // Q_BLOCK
    topk = min(TOPK_MAX, seq // 4)
    ca, sa, ci, si = tables
    qrow = lambda col: (lambda b, j: (b * nb + j, col // BRANCH_WIDTH))
    krow = lambda col: (lambda b, j: (b, col // LANES))
    tab = pl.BlockSpec((1, seq, LANES), lambda b, j: (b, 0, 0))
    vec = pl.BlockSpec((1, LANES), lambda b, j: (0, 0))
    return pl.pallas_call(
        functools.partial(_dsa_kernel, seq=seq, topk=topk),
        grid=(batch, nb),
        in_specs=[pl.BlockSpec((Q_BLOCK, BRANCH_WIDTH), qrow(COL_AT_Q)),
                  pl.BlockSpec((Q_BLOCK, BRANCH_WIDTH), qrow(COL_AT_G)),
                  pl.BlockSpec((Q_BLOCK, BRANCH_WIDTH), qrow(COL_IX_Q)),
                  pl.BlockSpec((seq, LANES), krow(COL_AT_K)),
                  pl.BlockSpec((seq, LANES), krow(COL_AT_V)),
                  pl.BlockSpec((seq, LANES), krow(COL_IX_KW)),
                  tab, tab, tab, tab, vec, vec],
        out_specs=pl.BlockSpec((Q_BLOCK, BRANCH_WIDTH), lambda b, j: (b * nb + j, 0)),
        out_shape=jax.ShapeDtypeStruct((batch * seq, BRANCH_WIDTH), F32),
        scratch_shapes=[pltpu.VMEM((seq, LANES), BF16), pltpu.VMEM((seq, LANES), BF16),
                        pltpu.VMEM((seq, LANES), BF16),
                        pltpu.VMEM((Q_BLOCK, seq), jnp.int32), pltpu.VMEM((Q_BLOCK, seq), F32)],
        compiler_params=_cparams("parallel", "arbitrary"),
        name="dsa_mixer",
    )(proj, proj, proj, proj, proj, proj, ca, sa, ci, si, qn_g, kn_g)


def _s5prep_kernel(are_ref, aim_ref, ldt_ref, btr_ref, bti_ref, cr_ref, ci_ref,
                   t_ref, min_ref, nout_ref, al_ref):
    Lb, G16 = S5_BLOCK, S5_GROUP
    a_re, a_im = are_ref[0], aim_ref[0]
    dt = jnp.exp(ldt_ref[0])
    mag = jnp.exp(a_re * dt)
    ang = a_im * dt
    ab_r, ab_i = mag * jnp.cos(ang), mag * jnp.sin(ang)
    nr, ni = ab_r - 1.0, ab_i
    den = a_re * a_re + a_im * a_im
    fr = (nr * a_re + ni * a_im) / den
    fi = (ni * a_re - nr * a_im) / den
    bt_r, bt_i = btr_ref[0], bti_ref[0]
    bb_r = fr * bt_r - fi * bt_i
    bb_i = fr * bt_i + fi * bt_r
    c_r, c_i = cr_ref[0], ci_ref[0]

    pw_r, pw_i = [jnp.ones_like(ab_r)], [jnp.zeros_like(ab_r)]
    for _ in range(Lb):
        pr, pi = pw_r[-1], pw_i[-1]
        pw_r.append(pr * ab_r - pi * ab_i)
        pw_i.append(pr * ab_i + pi * ab_r)
    tab_r = jnp.concatenate(pw_r[:Lb], axis=0)
    tab_i = jnp.concatenate(pw_i[:Lb], axis=0)

    g1_r = jnp.concatenate([c_r * bb_r[i:i + 1] - c_i * bb_i[i:i + 1] for i in range(G16)], axis=0)
    g1_i = jnp.concatenate([c_r * bb_i[i:i + 1] + c_i * bb_r[i:i + 1] for i in range(G16)], axis=0)

    def shifted(tab, s):
        if s == 0:
            return tab
        return jnp.concatenate([jnp.zeros((s, tab.shape[1]), F32), tab[:Lb - s]], axis=0)

    w_r = jnp.concatenate([shifted(tab_r, s) for s in range(Lb)], axis=0)
    w_i = jnp.concatenate([shifted(tab_i, s) for s in range(Lb)], axis=0)
    hp = lax.Precision.HIGHEST
    t_ref[0] = (lax.dot_general(g1_r, w_r, _NT, precision=hp, preferred_element_type=F32)
                - lax.dot_general(g1_i, w_i, _NT, precision=hp, preferred_element_type=F32))

    m_r = jnp.concatenate([pw_r[Lb - 1 - s] * bb_r - pw_i[Lb - 1 - s] * bb_i for s in range(Lb)], axis=0)
    m_i = jnp.concatenate([pw_r[Lb - 1 - s] * bb_i + pw_i[Lb - 1 - s] * bb_r for s in range(Lb)], axis=0)
    min_ref[0] = jnp.concatenate([m_r, m_i], axis=1)
    n_r = jnp.concatenate([c_r * pw_r[t + 1] - c_i * pw_i[t + 1] for t in range(Lb)], axis=0)
    n_i = jnp.concatenate([c_r * pw_i[t + 1] + c_i * pw_r[t + 1] for t in range(Lb)], axis=0)
    nout_ref[0] = jnp.concatenate([n_r, -n_i], axis=1)
    al_r, al_i = pw_r[Lb], pw_i[Lb]
    al_ref[0] = jnp.concatenate([jnp.concatenate([al_r, al_r], axis=1),
                                 jnp.concatenate([-al_i, al_i], axis=1),
                                 jnp.zeros((6, 2 * S5_STATE), F32)], axis=0)


def _s5_prepare(a_re, a_im, log_dt, b_re, b_im, c_re, c_im):
    G, P, K = S5_GROUPS, S5_STATE, S5_BLOCK * S5_GROUP
    g3 = lambda shape: pl.BlockSpec((1,) + shape, lambda g: (g, 0, 0))
    t_flat, m_in, n_out, al = pl.pallas_call(
        _s5prep_kernel,
        grid=(G,),
        in_specs=[g3((1, P)), g3((1, P)), g3((1, 1)), g3((S5_GROUP, P)), g3((S5_GROUP, P)),
                  g3((S5_GROUP, P)), g3((S5_GROUP, P))],
        out_specs=[g3((K, K)), g3((K, 2 * P)), g3((K, 2 * P)), g3((8, 2 * P))],
        out_shape=[jax.ShapeDtypeStruct((G, K, K), F32), jax.ShapeDtypeStruct((G, K, 2 * P), F32),
                   jax.ShapeDtypeStruct((G, K, 2 * P), F32), jax.ShapeDtypeStruct((G, 8, 2 * P), F32)],
        compiler_params=_cparams("parallel"),
        name="s5_prepare",
    )(a_re.reshape(G, 1, P), a_im.reshape(G, 1, P), log_dt.reshape(G, 1, 1),
      jnp.swapaxes(b_re, 1, 2), jnp.swapaxes(b_im, 1, 2), c_re, c_im)
    t_mat = t_flat.reshape(G, S5_GROUP, S5_GROUP, S5_BLOCK, S5_BLOCK).transpose(0, 3, 1, 4, 2)
    return t_mat.reshape(G, K, K).astype(BF16), m_in.astype(BF16), n_out.astype(BF16), al


def _s5_kernel(u_ref, t_ref, min_ref, nout_ref, al_ref, y_ref, inj_s, xp_s, *, batch, nblk):
    u = u_ref[0]
    y_ref[0] = jnp.dot(u, t_ref[0], preferred_element_type=F32)
    inj_s[...] = jnp.dot(u, min_ref[0], preferred_element_type=F32)
    a_same = al_ref[0, 0:1, :]
    a_cross = al_ref[0, 1:2, :]

    def step(jb, x):
        rows = pl.ds(pl.multiple_of(jb * batch, batch), batch)
        xp_s[rows, :] = x
        return x * a_same + pltpu.roll(x, S5_STATE, 1) * a_cross + inj_s[rows, :]

    lax.fori_loop(0, nblk, step, jnp.zeros((batch, 2 * S5_STATE), F32))
    y_ref[0] += lax.dot_general(xp_s[...].astype(BF16), nout_ref[0], _NT, preferred_element_type=F32)


def _s5_scan(proj3, ops, batch, seq):
    t_mat, m_in, n_out, al = ops
    G, P, K = S5_GROUPS, S5_STATE, S5_BLOCK * S5_GROUP
    nblk = seq // S5_BLOCK
    R = nblk * batch
    u = proj3[:, :, COL_S5_U:COL_S5_U + BRANCH_WIDTH]
    u = u.reshape(batch, nblk, S5_BLOCK, G, S5_GROUP).transpose(3, 1, 0, 2, 4).reshape(G, R, K).astype(BF16)
    g3 = lambda shape: pl.BlockSpec((1,) + shape, lambda g: (g, 0, 0))
    y = pl.pallas_call(
        functools.partial(_s5_kernel, batch=batch, nblk=nblk),
        grid=(G,),
        in_specs=[g3((R, K)), g3((K, K)), g3((K, 2 * P)), g3((K, 2 * P)), g3((8, 2 * P))],
        out_specs=g3((R, K)),
        out_shape=jax.ShapeDtypeStruct((G, R, K), F32),
        scratch_shapes=[pltpu.VMEM((R, 2 * P), F32), pltpu.VMEM((R, 2 * P), F32)],
        compiler_params=_cparams("parallel"),
        name="s5_scan",
    )(u, t_mat, m_in, n_out, al)
    y = y.reshape(G, nblk, batch, S5_BLOCK, S5_GROUP).transpose(2, 1, 3, 0, 4)
    return y.reshape(batch * seq, BRANCH_WIDTH)


def _merge_kernel(x_ref, gate_ref, ya_ref, yb_ref, y5_ref, u_ref, sg_ref, mg_ref,
                  dsk_ref, gw_ref, gb_ref, wb_ref, wo_ref, o_ref):
    W, D = BRANCH_WIDTH, D_MODEL
    y = y5_ref[...] + dsk_ref[...] * u_ref[...]
    y = jax.nn.gelu(y)
    glu = jnp.dot(y.astype(BF16), gw_ref[...], preferred_element_type=F32) + gb_ref[...]
    yc = y * _sigmoid(glu) * _silu(sg_ref[...])
    merged = jnp.zeros((x_ref.shape[0], D), F32)
    for n, yn in enumerate((ya_ref[...], yb_ref[...], yc)):
        yd = jnp.dot(yn.astype(BF16), wb_ref[n], preferred_element_type=F32)
        merged = merged + _sigmoid(mg_ref[:, n * D:(n + 1) * D]) * yd
    o_ref[...] = x_ref[...] + gate_ref[0] * jnp.dot(merged.astype(BF16), wo_ref[...],
                                                    preferred_element_type=F32)


def _merge(x2, gate, ya, yb, y5, proj, d_skip, glu_w, glu_b, w_branch, w_out, seq):
    M, D = x2.shape
    W = BRANCH_WIDTH
    tm = 256
    per_batch = seq // tm
    row = lambda cols, colblk: pl.BlockSpec((tm, cols), lambda i: (i, colblk))
    full = lambda shape: pl.BlockSpec(shape, lambda i: (0,) * len(shape))
    return pl.pallas_call(
        _merge_kernel,
        grid=(M // tm,),
        in_specs=[row(D, 0),
                  pl.BlockSpec((1, 1, D), lambda i: (i // per_batch, 0, 0)),
                  row(W, 0), row(W, 0), row(W, 0),
                  row(W, COL_S5_U // W), row(W, COL_S5_G // W), row(N_BRANCH * D, 0),
                  full((1, W)), full((W, W)), full((1, W)), full((N_BRANCH, W, D)), full((D, D))],
        out_specs=row(D, 0),
        out_shape=jax.ShapeDtypeStruct((M, D), F32),
        compiler_params=_cparams("parallel"),
        name="merge_out",
    )(x2, gate, ya, yb, y5, proj, proj, proj, d_skip, glu_w, glu_b, w_branch, w_out)


def kernel(x, c, positions, ada_w, ada_b, norm_g, w_in, hg_lb_logits, hg_onorm_g, at_qnorm_g,
           at_knorm_g, s5_a_re, s5_a_im, s5_log_dt, s5_b_re, s5_b_im, s5_c_re, s5_c_im, s5_d,
           s5_glu_w, s5_glu_b, w_branch, w_out):
    B, S, D = x.shape
    L = ada_w.shape[0]
    lb_all = _lower_bounds(hg_lb_logits.astype(F32))
    mod = _modulation(c, ada_w, ada_b)
    tables = _rope_tables(positions)
    x2 = x.reshape(B * S, D)
    for l in range(L):
        shift = mod[l, :, None, 0:D]
        scale = mod[l, :, None, D:2 * D]
        gate = mod[l, :, None, 2 * D:3 * D]
        proj = _projection(x2, norm_g[l][None, :], shift, scale, _permute_w_in(w_in[l]), S)
        ya = _hgrn(proj, lb_all[l][None, :], hg_onorm_g[l][None, :], B, S)
        yb = _dsa(proj, tables, at_qnorm_g[l][None, :], at_knorm_g[l][None, :], B, S)
        ops = _s5_prepare(s5_a_re[l], s5_a_im[l], s5_log_dt[l], s5_b_re[l], s5_b_im[l],
                          s5_c_re[l], s5_c_im[l])
        y5 = _s5_scan(proj.reshape(B, S, N_PROJ), ops, B, S)
        x2 = _merge(x2, gate, ya, yb, y5, proj, s5_d[l][None, :], s5_glu_w[l].astype(BF16),
                    s5_glu_b[l][None, :], w_branch[l].astype(BF16), w_out[l].astype(BF16), S)
    return x2.reshape(B, S, D)
```

```python
import functools
import math

import jax
import jax.numpy as jnp
from jax import lax
from jax.experimental import pallas as pl
from jax.experimental.pallas import tpu as pltpu

F32 = jnp.float32
BF16 = jnp.bfloat16

D_MODEL = 1024
DEPTH = 4
BRANCH_WIDTH = 512
N_BRANCH = 3
EPS = 1e-6
NEG_BIG = -1e30
HG_HEADS = 4
HG_DIM = 128
AT_HEADS = 4
AT_DIM = 128
IDX_HEADS = 8
IDX_DIM = 64
TOPK_MAX = 256
Q_BLOCK = 128
ROPE_THETA = 10000.0
S5_GROUP = 16
S5_GROUPS = BRANCH_WIDTH // S5_GROUP
S5_STATE = 64
N_IN = 8008

LANES = 128
HG_CHUNK = 128
HG_SUB = 16
S5_BLOCK = 16
KEY_GROUP = 256
VMEM_LIMIT = 52 * 1024 * 1024

REF_ALIGNED = 3840
REF_TAIL = 3912
COL_S5_U = 0
COL_S5_G = 512
COL_MERGE = 1024
COL_HG_Q = 4096
COL_HG_F = 4608
COL_HG_I = 5120
COL_HG_G = 5632
COL_AT_Q = 6144
COL_AT_K = 6656
COL_AT_V = 6784
COL_AT_G = 6912
COL_IX_Q = 7424
COL_IX_KW = 7936
N_PROJ = 8064
PROJ_TN = 1152
assert COL_HG_F % PROJ_TN == 0

_NT = (((1,), (1,)), ((), ()))
_TN = (((0,), (0,)), ((), ()))


def _cparams(*sem):
    return pltpu.CompilerParams(dimension_semantics=sem, vmem_limit_bytes=VMEM_LIMIT)


def _sigmoid(x):
    return 1.0 / (1.0 + jnp.exp(-x))


def _silu(x):
    return x * _sigmoid(x)


def _lb_kernel(z_ref, o_ref):
    z = z_ref[...]
    e = jnp.exp(z - jnp.max(z, axis=0, keepdims=True))
    p = e / jnp.sum(e, axis=0, keepdims=True)
    acc = jnp.zeros_like(p[0:1])
    for l in range(z.shape[0]):
        acc = acc + p[l:l + 1]
        o_ref[l:l + 1, :] = acc - p[0:1]


def _lower_bounds(logits):
    return pl.pallas_call(
        _lb_kernel, out_shape=jax.ShapeDtypeStruct(logits.shape, F32), name="hg_lower_bounds",
    )(logits)


def _mod_kernel(c_ref, w_ref, b_ref, o_ref):
    c = c_ref[...]
    o_ref[0] = jnp.dot(_silu(c), w_ref[0], precision=lax.Precision.HIGHEST,
                       preferred_element_type=F32) + b_ref[0]


def _modulation(c, ada_w, ada_b):
    L, D, N = ada_w.shape
    B = c.shape[0]
    tn = 512
    return pl.pallas_call(
        _mod_kernel,
        grid=(L, N // tn),
        in_specs=[pl.BlockSpec((B, D), lambda l, n: (0, 0)),
                  pl.BlockSpec((1, D, tn), lambda l, n: (l, 0, n)),
                  pl.BlockSpec((1, 1, tn), lambda l, n: (l, 0, n))],
        out_specs=pl.BlockSpec((1, B, tn), lambda l, n: (l, 0, n)),
        out_shape=jax.ShapeDtypeStruct((L, B, N), F32),
        compiler_params=_cparams("parallel", "parallel"),
        name="adaln_modulation",
    )(c, ada_w, ada_b.reshape(L, 1, N))


def _wprep_kernel(w_ref, o_ref):
    tail = N_IN - REF_TAIL
    o_ref[0, :, 0:tail] = w_ref[0, :, REF_TAIL:N_IN].astype(BF16)
    o_ref[0, :, tail:tail + REF_ALIGNED] = w_ref[0, :, 0:REF_ALIGNED].astype(BF16)
    kw = w_ref[0, :, REF_ALIGNED:REF_ALIGNED + LANES]
    lane = lax.broadcasted_iota(jnp.int32, kw.shape, 1)
    o_ref[0, :, COL_IX_KW:N_PROJ] = jnp.where(lane < REF_TAIL - REF_ALIGNED, kw, 0.0).astype(BF16)


def _permute_w_in(w_in):
    L, D, N = w_in.shape
    tr = 128
    return pl.pallas_call(
        _wprep_kernel,
        grid=(L, D // tr),
        in_specs=[pl.BlockSpec((1, tr, N), lambda l, r: (l, r, 0))],
        out_specs=pl.BlockSpec((1, tr, N_PROJ), lambda l, r: (l, r, 0)),
        out_shape=jax.ShapeDtypeStruct((L, D, N_PROJ), BF16),
        compiler_params=_cparams("parallel", "parallel"),
        name="w_in_prepare",
    )(w_in)


def _proj_kernel(x_ref, g_ref, shift_ref, scale_ref, w_ref, o_ref, f_ref, h_ref):
    j = pl.program_id(1)

    @pl.when(j == 0)
    def _():
        x = x_ref[...]
        r = lax.rsqrt(jnp.mean(x * x, axis=-1, keepdims=True) + EPS)
        h = (x * r) * g_ref[...] * (1.0 + scale_ref[0]) + shift_ref[0]
        h_ref[...] = h.astype(BF16)

    acc = jnp.dot(h_ref[...], w_ref[0], preferred_element_type=F32)
    o_ref[...] = acc.astype(BF16)

    @pl.when(j == COL_HG_F // PROJ_TN)
    def _():
        f_ref[...] = acc[:, 0:BRANCH_WIDTH]


def _projection(x2, norm_g, shift, scale, w_all, layer, seq):
    M, D = x2.shape
    tm = min(1024, seq)
    per_batch = seq // tm
    return pl.pallas_call(
        _proj_kernel,
        grid=(M // tm, N_PROJ // PROJ_TN),
        in_specs=[pl.BlockSpec((tm, D), lambda i, j: (i, 0)),
                  pl.BlockSpec((1, D), lambda i, j: (0, 0)),
                  pl.BlockSpec((1, 1, D), lambda i, j: (i // per_batch, 0, 0)),
                  pl.BlockSpec((1, 1, D), lambda i, j: (i // per_batch, 0, 0)),
                  pl.BlockSpec((1, D, PROJ_TN), lambda i, j: (layer, 0, j))],
        out_specs=[pl.BlockSpec((tm, PROJ_TN), lambda i, j: (i, j)),
                   pl.BlockSpec((tm, BRANCH_WIDTH), lambda i, j: (i, 0))],
        out_shape=[jax.ShapeDtypeStruct((M, N_PROJ), BF16),
                   jax.ShapeDtypeStruct((M, BRANCH_WIDTH), F32)],
        scratch_shapes=[pltpu.VMEM((tm, D), BF16)],
        compiler_params=_cparams("parallel", "arbitrary"),
        name="norm_in_proj",
    )(x2, norm_g, shift, scale, w_all)


def _split3(x):
    hi = x.astype(BF16)
    r1 = x - hi.astype(F32)
    mid = r1.astype(BF16)
    lo = (r1 - mid.astype(F32)).astype(BF16)
    return hi, mid, lo


def _hgrn_kernel(q_ref, f_ref, i_ref, g_ref, lb_ref, on_ref, o_ref, state_ref, *, rows):
    C, c = HG_CHUNK, HG_SUB
    nsub = C // c

    @pl.when(pl.program_id(2) == 0)
    def _():
        state_ref[...] = jnp.zeros_like(state_ref)

    lb = lb_ref[...]
    row_id = lax.broadcasted_iota(jnp.int32, (C, C), 0)
    col_id = lax.broadcasted_iota(jnp.int32, (C, C), 1)
    tril = jnp.where(col_id <= row_id, 1.0, 0.0).astype(BF16)
    sub_row = lax.broadcasted_iota(jnp.int32, (c, C), 0)
    sub_col = lax.broadcasted_iota(jnp.int32, (c, C), 1)

    for ci in range(rows // C):
        sl = slice(ci * C, (ci + 1) * C)
        q = q_ref[sl, :].astype(F32)
        v16 = i_ref[sl, :]
        fg = lb + (1.0 - lb) * _sigmoid(f_ref[sl, :])
        lf = jnp.log(jnp.maximum(fg, 1e-30))
        k = 1.0 - fg
        hi, mid, lo = _split3(lf)
        b = (jnp.dot(tril, hi, preferred_element_type=F32)
             + jnp.dot(tril, mid, preferred_element_type=F32)
             + jnp.dot(tril, lo, preferred_element_type=F32))
        b_last = b[C - 1:C, :]

        state_t = state_ref[...]
        q0 = (q * jnp.exp(b)).astype(BF16)
        o = lax.dot_general(q0, state_t.astype(BF16), _NT, preferred_element_type=F32)

        row_blocks = []
        for t in range(nsub):
            r0 = t * c
            bt = b[r0:r0 + c, :]
            qt = q[r0:r0 + c, :]
            diag = jnp.zeros((c, C), F32)
            for s in range(c):
                e = jnp.exp(jnp.minimum(bt - b[r0 + s:r0 + s + 1, :], 0.0))
                col = jnp.sum(qt * e * k[r0 + s:r0 + s + 1, :], axis=-1, keepdims=True)
                diag = jnp.where(sub_col == r0 + s, col, diag)
            diag = jnp.where(sub_col - r0 <= sub_row, diag, 0.0)
            if t == 0:
                row_blocks.append(diag)
                continue
            r_t = b[r0 - 1:r0, :]
            q_t = (qt * jnp.exp(bt - r_t)).astype(BF16)
            k_t = (k * jnp.exp(jnp.minimum(r_t - b, 0.0))).astype(BF16)
            off = lax.dot_general(q_t, k_t, _NT, preferred_element_type=F32)
            row_blocks.append(jnp.where(sub_col < r0, off, diag))
        scores = jnp.concatenate(row_blocks, axis=0)
        o = o + jnp.dot(scores.astype(BF16), v16, preferred_element_type=F32)

        k_dec = (k * jnp.exp(b_last - b)).astype(BF16)
        state_ref[...] = (state_t * jnp.exp(b_last)
                          + lax.dot_general(v16, k_dec, _TN, preferred_element_type=F32))

        r = lax.rsqrt(jnp.mean(o * o, axis=-1, keepdims=True) + EPS)
        o_ref[sl, :] = ((o * r) * on_ref[...] * _silu(g_ref[sl, :].astype(F32))).astype(BF16)


def _hgrn(proj, hgf, lb, onorm_g, batch, seq):
    rows = min(512, seq)
    nr = seq // rows
    cb = lambda base: (lambda b, h, r: (b * nr + r, base // HG_DIM + h))
    return pl.pallas_call(
        functools.partial(_hgrn_kernel, rows=rows),
        grid=(batch, HG_HEADS, nr),
        in_specs=[pl.BlockSpec((rows, HG_DIM), cb(COL_HG_Q)),
                  pl.BlockSpec((rows, HG_DIM), cb(0)),
                  pl.BlockSpec((rows, HG_DIM), cb(COL_HG_I)),
                  pl.BlockSpec((rows, HG_DIM), cb(COL_HG_G)),
                  pl.BlockSpec((1, HG_DIM), lambda b, h, r: (0, h)),
                  pl.BlockSpec((1, HG_DIM), lambda b, h, r: (0, 0))],
        out_specs=pl.BlockSpec((rows, HG_DIM), lambda b, h, r: (b * nr + r, h)),
        out_shape=jax.ShapeDtypeStruct((batch * seq, BRANCH_WIDTH), BF16),
        scratch_shapes=[pltpu.VMEM((HG_DIM, HG_DIM), F32)],
        compiler_params=_cparams("parallel", "parallel", "arbitrary"),
        name="hgrn2_mixer",
    )(proj, hgf, proj, proj, lb, onorm_g)


def _rope_tables(positions):
    pos = positions.astype(F32)[..., None]

    def tables(dim, reps):
        inv = ROPE_THETA ** (-jnp.arange(0, dim, 2, dtype=F32) / dim)
        ang = pos * inv
        c, s = jnp.cos(ang), jnp.sin(ang)
        return (jnp.concatenate([c, c] * reps, axis=-1),
                jnp.concatenate([-s, s] * reps, axis=-1))

    return tables(AT_DIM, 1) + tables(IDX_DIM, LANES // IDX_DIM)


def _rope_full(x, cos, sin_signed):
    return x * cos + pltpu.roll(x, AT_DIM // 2, 1) * sin_signed


def _rope_idx(x, cos, sin_signed, first_half):
    h = IDX_DIM // 2
    partner = jnp.where(first_half, pltpu.roll(x, LANES - h, 1), pltpu.roll(x, h, 1))
    return x * cos + partner * sin_signed


def _col_reduce(x, op):
    part = 64
    if x.shape[0] > part:
        x = op(x.reshape(x.shape[0] // part, part, x.shape[1]), axis=0)
    return op(x, axis=0, keepdims=True)


def _dsa_block(width, j, q_ref, g_refs, iq_refs, w_rows, ca, sa, ci, si, qn_ref, o_ref,
               k_s, vt_s, ik_s, key_s, val_s, *, topk, first_half, low_group):
    QB = Q_BLOCK
    kk_s, ikk_s = k_s.at[0:width, :], ik_s.at[0:width, :]
    key_w, val_w = key_s.at[0:width, :], val_s.at[0:width, :]

    isc = jnp.zeros((width, QB), F32)
    per_slab = LANES // IDX_DIM
    for m in range(IDX_HEADS // per_slab):
        iq_ref = iq_refs[m // 2]
        off = (m % 2) * LANES
        xr = _rope_idx(iq_ref[:, off:off + LANES].astype(F32), ci, si, first_half)
        for par in range(per_slab):
            h = m * per_slab + par
            xm = jnp.where(low_group if par == 0 else jnp.logical_not(low_group), xr, 0.0)
            rel = lax.dot_general(ikk_s[...], xm.astype(BF16), _NT, preferred_element_type=F32)
            isc = isc + jnp.maximum(rel, 0.0) * w_rows[IDX_DIM + h:IDX_DIM + h + 1, :]
    qpos = j * QB + lax.broadcasted_iota(jnp.int32, (1, QB), 1)
    kpos = lax.broadcasted_iota(jnp.int32, (width, 1), 0)
    causal = kpos <= qpos
    isc = jnp.where(causal, isc, NEG_BIG)
    bits = pltpu.bitcast(isc, jnp.int32)
    key_w[...] = jnp.where(bits < 0, bits ^ jnp.int32(0x7FFFFFFF), bits)

    kf = float(topk)

    def count_ge(cand):
        return _col_reduce(jnp.where(key_w[...] >= cand, 1.0, 0.0), jnp.sum)

    zero = jnp.zeros((1, QB), jnp.int32)
    ans0 = jnp.where(count_ge(zero) >= kf, zero, jnp.full((1, QB), -2**31, jnp.int32))

    def bisect(i, ans):
        cand = ans + (jnp.int32(1) << (30 - i))
        return jnp.where(count_ge(cand) >= kf, cand, ans)

    thr = lax.fori_loop(0, 31, bisect, ans0)

    key = key_w[...]
    gt = key > thr
    eq = key == thr
    need = kf - _col_reduce(jnp.where(gt, 1.0, 0.0), jnp.sum)
    rr = lax.broadcasted_iota(jnp.int32, (LANES, LANES), 0)
    cc = lax.broadcasted_iota(jnp.int32, (LANES, LANES), 1)
    lower = jnp.where(cc <= rr, 1.0, 0.0).astype(BF16)
    seen = jnp.zeros((1, QB), F32)
    for cidx in range(width // LANES):
        sl = slice(cidx * LANES, (cidx + 1) * LANES)
        eq_c = jnp.where(eq[sl, :], 1.0, 0.0)
        rank = jnp.dot(lower, eq_c.astype(BF16), preferred_element_type=F32) + seen
        take = jnp.where(gt[sl, :], 1.0, jnp.where(rank <= need, eq_c, 0.0))
        val_w[sl, :] = jnp.where(causal[sl, :], take, 0.0)
        seen = rank[LANES - 1:LANES, :]

    scale = 1.0 / math.sqrt(AT_DIM)
    valid = val_w[...] > 0.0
    for h in range(AT_HEADS):
        hs = slice(h * AT_DIM, (h + 1) * AT_DIM)
        qh = q_ref[:, hs].astype(F32)
        r = lax.rsqrt(jnp.mean(qh * qh, axis=-1, keepdims=True) + EPS)
        qh = _rope_full((qh * r) * qn_ref[...], ca, sa) * scale
        logits = lax.dot_general(kk_s[...], qh.astype(BF16), _NT, preferred_element_type=F32)
        logits = jnp.where(valid, logits, NEG_BIG)
        p = jnp.exp(logits - _col_reduce(logits, jnp.max))
        denom = _col_reduce(p, jnp.sum)
        oh_t = jnp.dot(vt_s[:, 0:width], p.astype(BF16), preferred_element_type=F32) / denom
        g_ref = g_refs[h // 2]
        goff = (h % 2) * AT_DIM
        o_ref[:, hs] = (oh_t.T * _silu(g_ref[:, goff:goff + AT_DIM].astype(F32))).astype(BF16)


def _dsa_kernel(q_ref, g0_ref, g1_ref, iq0_ref, iq1_ref, k_ref, v_ref, ikw_ref,
                ca_ref, sa_ref, ci_ref, si_ref, qn_ref, kn_ref, o_ref,
                k_s, vt_s, ik_s, key_s, val_s, *, seq, topk):
    j = pl.program_id(1)
    QB = Q_BLOCK
    lane = lax.broadcasted_iota(jnp.int32, (1, LANES), 1)
    first_half = (lane % IDX_DIM) < (IDX_DIM // 2)
    low_group = lane < IDX_DIM

    @pl.when(j == 0)
    def _():
        kk = k_ref[...].astype(F32)
        r = lax.rsqrt(jnp.mean(kk * kk, axis=-1, keepdims=True) + EPS)
        kk = (kk * r) * kn_ref[...]
        k_s[...] = _rope_full(kk, ca_ref[0], sa_ref[0]).astype(BF16)
        vt_s[...] = v_ref[...].astype(F32).T.astype(BF16)
        ik = _rope_idx(ikw_ref[...].astype(F32), ci_ref[0], si_ref[0], first_half)
        ik_s[...] = jnp.where(low_group, ik, pltpu.roll(ik, IDX_DIM, 1)).astype(BF16)

    rows = pl.ds(pl.multiple_of(j * QB, QB), QB)
    ca, sa = ca_ref[0, rows, :], sa_ref[0, rows, :]
    ci, si = ci_ref[0, rows, :], si_ref[0, rows, :]
    w_rows = ikw_ref[rows, :].astype(F32).T

    group = min(KEY_GROUP, seq)
    per_group = group // QB
    for n in range(1, seq // group + 1):
        @pl.when(j // per_group + 1 == n)
        def _(n=n):
            _dsa_block(n * group, j, q_ref, (g0_ref, g1_ref), (iq0_ref, iq1_ref), w_rows,
                       ca, sa, ci, si, qn_ref, o_ref, k_s, vt_s, ik_s, key_s, val_s,
                       topk=topk, first_half=first_half, low_group=low_group)


def _dsa(proj, tables, qn_g, kn_g, batch, seq):
    nb = seq // Q_BLOCK
    topk = min(TOPK_MAX, seq // 4)
    ca, sa, ci, si = tables
    half = BRANCH_WIDTH // 2
    qrow = lambda width, col: pl.BlockSpec((Q_BLOCK, width), lambda b, j: (b * nb + j, col // width))
    krow = lambda col: pl.BlockSpec((seq, LANES), lambda b, j: (b, col // LANES))
    tab = pl.BlockSpec((1, seq, LANES), lambda b, j: (b, 0, 0))
    vec = pl.BlockSpec((1, LANES), lambda b, j: (0, 0))
    return pl.pallas_call(
        functools.partial(_dsa_kernel, seq=seq, topk=topk),
        grid=(batch, nb),
        in_specs=[qrow(BRANCH_WIDTH, COL_AT_Q),
                  qrow(half, COL_AT_G), qrow(half, COL_AT_G + half),
                  qrow(half, COL_IX_Q), qrow(half, COL_IX_Q + half),
                  krow(COL_AT_K), krow(COL_AT_V), krow(COL_IX_KW),
                  tab, tab, tab, tab, vec, vec],
        out_specs=pl.BlockSpec((Q_BLOCK, BRANCH_WIDTH), lambda b, j: (b * nb + j, 0)),
        out_shape=jax.ShapeDtypeStruct((batch * seq, BRANCH_WIDTH), BF16),
        scratch_shapes=[pltpu.VMEM((seq, LANES), BF16), pltpu.VMEM((LANES, seq), BF16),
                        pltpu.VMEM((seq, LANES), BF16),
                        pltpu.VMEM((seq, Q_BLOCK), jnp.int32), pltpu.VMEM((seq, Q_BLOCK), F32)],
        compiler_params=_cparams("parallel", "arbitrary"),
        name="dsa_mixer",
    )(proj, proj, proj, proj, proj, proj, proj, proj, ca, sa, ci, si, qn_g, kn_g)


def _s5prep_kernel(are_ref, aim_ref, ldt_ref, btr_ref, bti_ref, cr_ref, ci_ref,
                   t_ref, min_ref, nout_ref, al_ref):
    Lb, G16 = S5_BLOCK, S5_GROUP
    a_re, a_im = are_ref[0], aim_ref[0]
    dt = jnp.exp(ldt_ref[0])
    mag = jnp.exp(a_re * dt)
    ang = a_im * dt
    ab_r, ab_i = mag * jnp.cos(ang), mag * jnp.sin(ang)
    nr, ni = ab_r - 1.0, ab_i
    den = a_re * a_re + a_im * a_im
    fr = (nr * a_re + ni * a_im) / den
    fi = (ni * a_re - nr * a_im) / den
    bt_r, bt_i = btr_ref[0], bti_ref[0]
    bb_r = fr * bt_r - fi * bt_i
    bb_i = fr * bt_i + fi * bt_r
    c_r, c_i = cr_ref[0], ci_ref[0]

    pw_r, pw_i = [jnp.ones_like(ab_r)], [jnp.zeros_like(ab_r)]
    for _ in range(Lb):
        pr, pi = pw_r[-1], pw_i[-1]
        pw_r.append(pr * ab_r - pi * ab_i)
        pw_i.append(pr * ab_i + pi * ab_r)
    tab_r = jnp.concatenate(pw_r[:Lb], axis=0)
    tab_i = jnp.concatenate(pw_i[:Lb], axis=0)

    g1_r = jnp.concatenate([c_r * bb_r[i:i + 1] - c_i * bb_i[i:i + 1] for i in range(G16)], axis=0)
    g1_i = jnp.concatenate([c_r * bb_i[i:i + 1] + c_i * bb_r[i:i + 1] for i in range(G16)], axis=0)

    def shifted(tab, s):
        if s == 0:
            return tab
        return jnp.concatenate([jnp.zeros((s, tab.shape[1]), F32), tab[:Lb - s]], axis=0)

    w_r = jnp.concatenate([shifted(tab_r, s) for s in range(Lb)], axis=0)
    w_i = jnp.concatenate([shifted(tab_i, s) for s in range(Lb)], axis=0)
    hp = lax.Precision.HIGHEST
    t_ref[0] = (lax.dot_general(g1_r, w_r, _NT, precision=hp, preferred_element_type=F32)
                - lax.dot_general(g1_i, w_i, _NT, precision=hp, preferred_element_type=F32))

    m_r = jnp.concatenate([pw_r[Lb - 1 - s] * bb_r - pw_i[Lb - 1 - s] * bb_i for s in range(Lb)], axis=0)
    m_i = jnp.concatenate([pw_r[Lb - 1 - s] * bb_i + pw_i[Lb - 1 - s] * bb_r for s in range(Lb)], axis=0)
    min_ref[0] = jnp.concatenate([m_r, m_i, m_i, m_r], axis=1)
    n_r = jnp.concatenate([c_r * pw_r[t + 1] - c_i * pw_i[t + 1] for t in range(Lb)], axis=0)
    n_i = jnp.concatenate([c_r * pw_i[t + 1] + c_i * pw_r[t + 1] for t in range(Lb)], axis=0)
    nout_ref[0] = jnp.concatenate([n_r, -n_i], axis=1)
    al_r, al_i = pw_r[Lb], pw_i[Lb]
    al_ref[0] = jnp.concatenate([jnp.concatenate([al_r, al_r], axis=1),
                                 jnp.concatenate([-al_i, al_i], axis=1),
                                 jnp.zeros((6, 2 * S5_STATE), F32)], axis=0)


def _s5_prepare(a_re, a_im, log_dt, b_re, b_im, c_re, c_im):
    G, P, K = S5_GROUPS, S5_STATE, S5_BLOCK * S5_GROUP
    g3 = lambda shape: pl.BlockSpec((1,) + shape, lambda g: (g, 0, 0))
    t_flat, m_in, n_out, al = pl.pallas_call(
        _s5prep_kernel,
        grid=(G,),
        in_specs=[g3((1, P)), g3((1, P)), g3((1, 1)), g3((S5_GROUP, P)), g3((S5_GROUP, P)),
                  g3((S5_GROUP, P)), g3((S5_GROUP, P))],
        out_specs=[g3((K, K)), g3((K, 4 * P)), g3((K, 2 * P)), g3((8, 2 * P))],
        out_shape=[jax.ShapeDtypeStruct((G, K, K), F32), jax.ShapeDtypeStruct((G, K, 4 * P), F32),
                   jax.ShapeDtypeStruct((G, K, 2 * P), F32), jax.ShapeDtypeStruct((G, 8, 2 * P), F32)],
        compiler_params=_cparams("parallel"),
        name="s5_prepare",
    )(a_re.reshape(G, 1, P), a_im.reshape(G, 1, P), log_dt.reshape(G, 1, 1),
      jnp.swapaxes(b_re, 1, 2), jnp.swapaxes(b_im, 1, 2), c_re, c_im)
    t_mat = t_flat.reshape(G, S5_GROUP, S5_GROUP, S5_BLOCK, S5_BLOCK).transpose(0, 3, 1, 4, 2)
    return t_mat.reshape(G, K, K).astype(BF16), m_in.astype(BF16), n_out.astype(BF16), al


def _s5_kernel(u_ref, t_ref, min_ref, nout_ref, al_ref, y_ref, inj_s, xp_s, *, batch, nblk):
    P2 = 2 * S5_STATE
    u = u_ref[0]
    inj_s[...] = jnp.dot(u, min_ref[0], preferred_element_type=F32)
    a_same = al_ref[0, 0:1, :]
    a_cross = al_ref[0, 1:2, :]

    def step(jb, carry):
        x, xs = carry
        rows = pl.ds(pl.multiple_of(jb * batch, batch), batch)
        xp_s[rows, :] = x
        inj = inj_s[rows, :]
        return (x * a_same + xs * a_cross + inj[:, 0:P2],
                xs * a_same - x * a_cross + inj[:, P2:2 * P2])

    z = jnp.zeros((batch, P2), F32)
    lax.fori_loop(0, nblk, step, (z, z))
    y = jnp.dot(u, t_ref[0], preferred_element_type=F32)
    y = y + lax.dot_general(xp_s[...].astype(BF16), nout_ref[0], _NT, preferred_element_type=F32)
    y_ref[0] = y.astype(BF16)


def _s5_scan(proj3, ops, batch, seq):
    t_mat, m_in, n_out, al = ops
    G, P, K = S5_GROUPS, S5_STATE, S5_BLOCK * S5_GROUP
    nblk = seq // S5_BLOCK
    R = nblk * batch
    u = proj3[:, :, COL_S5_U:COL_S5_U + BRANCH_WIDTH]
    u = u.reshape(batch, nblk, S5_BLOCK, G, S5_GROUP).transpose(3, 1, 0, 2, 4).reshape(G, R, K)
    g3 = lambda shape: pl.BlockSpec((1,) + shape, lambda g: (g, 0, 0))
    y = pl.pallas_call(
        functools.partial(_s5_kernel, batch=batch, nblk=nblk),
        grid=(G,),
        in_specs=[g3((R, K)), g3((K, K)), g3((K, 4 * P)), g3((K, 2 * P)), g3((8, 2 * P))],
        out_specs=g3((R, K)),
        out_shape=jax.ShapeDtypeStruct((G, R, K), BF16),
        scratch_shapes=[pltpu.VMEM((R, 4 * P), F32), pltpu.VMEM((R, 2 * P), F32)],
        compiler_params=_cparams("parallel"),
        name="s5_scan",
    )(u, t_mat, m_in, n_out, al)
    y = y.reshape(G, nblk, batch, S5_BLOCK, S5_GROUP).transpose(2, 1, 3, 0, 4)
    return y.reshape(batch * seq, BRANCH_WIDTH)


def _merge_kernel(x_ref, gate_ref, ya_ref, yb_ref, y5_ref, u_ref, sg_ref, m0_ref, m1_ref, m2_ref,
                  dsk_ref, gw_ref, gb_ref, wb_ref, wo_ref, o_ref):
    y = y5_ref[...].astype(F32) + dsk_ref[...] * u_ref[...].astype(F32)
    y = jax.nn.gelu(y)
    glu = jnp.dot(y.astype(BF16), gw_ref[...], preferred_element_type=F32) + gb_ref[...]
    yc = (y * _sigmoid(glu) * _silu(sg_ref[...].astype(F32))).astype(BF16)
    merged = jnp.zeros(x_ref.shape, F32)
    for n, (yn, m_ref) in enumerate(((ya_ref[...], m0_ref), (yb_ref[...], m1_ref), (yc, m2_ref))):
        yd = jnp.dot(yn, wb_ref[n], preferred_element_type=F32)
        merged = merged + _sigmoid(m_ref[...].astype(F32)) * yd
    o_ref[...] = x_ref[...] + gate_ref[0] * jnp.dot(merged.astype(BF16), wo_ref[...],
                                                    preferred_element_type=F32)


def _merge(x2, gate, ya, yb, y5, proj, d_skip, glu_w, glu_b, w_branch, w_out, seq):
    M, D = x2.shape
    W = BRANCH_WIDTH
    tm = min(512, seq)
    per_batch = seq // tm
    row = lambda cols, colblk: pl.BlockSpec((tm, cols), lambda i: (i, colblk))
    full = lambda shape: pl.BlockSpec(shape, lambda i: (0,) * len(shape))
    return pl.pallas_call(
        _merge_kernel,
        grid=(M // tm,),
        in_specs=[row(D, 0),
                  pl.BlockSpec((1, 1, D), lambda i: (i // per_batch, 0, 0)),
                  row(W, 0), row(W, 0), row(W, 0),
                  row(W, COL_S5_U // W), row(W, COL_S5_G // W),
                  row(D, COL_MERGE // D), row(D, COL_MERGE // D + 1), row(D, COL_MERGE // D + 2),
                  full((1, W)), full((W, W)), full((1, W)), full((N_BRANCH, W, D)), full((D, D))],
        out_specs=row(D, 0),
        out_shape=jax.ShapeDtypeStruct((M, D), F32),
        compiler_params=_cparams("parallel"),
        name="merge_out",
    )(x2, gate, ya, yb, y5, proj, proj, proj, proj, proj, d_skip, glu_w, glu_b, w_branch, w_out)


def kernel(x, c, positions, ada_w, ada_b, norm_g, w_in, hg_lb_logits, hg_onorm_g, at_qnorm_g,
           at_knorm_g, s5_a_re, s5_a_im, s5_log_dt, s5_b_re, s5_b_im, s5_c_re, s5_c_im, s5_d,
           s5_glu_w, s5_glu_b, w_branch, w_out):
    B, S, D = x.shape
    L = ada_w.shape[0]
    lb_all = _lower_bounds(hg_lb_logits.astype(F32))
    mod = _modulation(c, ada_w, ada_b)
    w_all = _permute_w_in(w_in)
    tables = _rope_tables(positions)
    x2 = x.reshape(B * S, D)
    for l in range(L):
        shift = mod[l, :, None, 0:D]
        scale = mod[l, :, None, D:2 * D]
        gate = mod[l, :, None, 2 * D:3 * D]
        proj, hgf = _projection(x2, norm_g[l][None, :], shift, scale, w_all, l, S)
        ya = _hgrn(proj, hgf, lb_all[l][None, :], hg_onorm_g[l][None, :], B, S)
        yb = _dsa(proj, tables, at_qnorm_g[l][None, :], at_knorm_g[l][None, :], B, S)
        ops = _s5_prepare(s5_a_re[l], s5_a_im[l], s5_log_dt[l], s5_b_re[l], s5_b_im[l],
                          s5_c_re[l], s5_c_im[l])
        y5 = _s5_scan(proj.reshape(B, S, N_PROJ), ops, B, S)
        x2 = _merge(x2, gate, ya, yb, y5, proj, s5_d[l][None, :], s5_glu_w[l].astype(BF16),
                    s5_glu_b[l][None, :], w_branch[l].astype(BF16), w_out[l].astype(BF16), S)
    return x2.reshape(B, S, D)
```

```python
import functools
import math

import jax
import jax.numpy as jnp
import numpy as np
from jax import lax
from jax.experimental import pallas as pl
from jax.experimental.pallas import tpu as pltpu

F32 = jnp.float32
BF16 = jnp.bfloat16

D_MODEL = 1024
DEPTH = 4
BRANCH_WIDTH = 512
N_BRANCH = 3
EPS = 1e-6
NEG_BIG = -1e30
HG_HEADS = 4
HG_DIM = 128
AT_HEADS = 4
AT_DIM = 128
IDX_HEADS = 8
IDX_DIM = 64
TOPK_MAX = 256
Q_BLOCK = 128
ROPE_THETA = 10000.0
S5_GROUP = 16
S5_GROUPS = BRANCH_WIDTH // S5_GROUP
S5_STATE = 64
N_IN = 8008

LANES = 128
HG_CHUNK = 128
HG_SUB = 16
S5_BLOCK = 16
KEY_GROUP = 256
VMEM_LIMIT = 52 * 1024 * 1024

REF_ALIGNED = 3840
REF_TAIL = 3912
COL_S5_U = 0
COL_S5_G = 512
COL_MERGE = 1024
COL_HG_Q = 4096
COL_HG_F = 4608
COL_HG_I = 5120
COL_HG_G = 5632
COL_AT_Q = 6144
COL_AT_K = 6656
COL_AT_V = 6784
COL_AT_G = 6912
COL_IX_Q = 7424
COL_IX_KW = 7936
N_PROJ = 8064
PROJ_TN = 1152
assert COL_HG_F % PROJ_TN == 0 and COL_S5_U % PROJ_TN == 0

_NT = (((1,), (1,)), ((), ()))
_TN = (((0,), (0,)), ((), ()))


def _cparams(*sem):
    return pltpu.CompilerParams(dimension_semantics=sem, vmem_limit_bytes=VMEM_LIMIT)


def _sigmoid(x):
    return 1.0 / (1.0 + jnp.exp(-x))


def _silu(x):
    return x * _sigmoid(x)


def _lb_kernel(z_ref, o_ref):
    z = z_ref[...]
    e = jnp.exp(z - jnp.max(z, axis=0, keepdims=True))
    p = e / jnp.sum(e, axis=0, keepdims=True)
    acc = jnp.zeros_like(p[0:1])
    for l in range(z.shape[0]):
        acc = acc + p[l:l + 1]
        o_ref[l:l + 1, :] = acc - p[0:1]


def _lower_bounds(logits):
    return pl.pallas_call(
        _lb_kernel, out_shape=jax.ShapeDtypeStruct(logits.shape, F32), name="hg_lower_bounds",
    )(logits)


def _mod_kernel(c_ref, w_ref, b_ref, o_ref):
    c = c_ref[...]
    o_ref[0] = jnp.dot(_silu(c), w_ref[0], precision=lax.Precision.HIGHEST,
                       preferred_element_type=F32) + b_ref[0]


def _modulation(c, ada_w, ada_b):
    L, D, N = ada_w.shape
    B = c.shape[0]
    tn = 512
    return pl.pallas_call(
        _mod_kernel,
        grid=(L, N // tn),
        in_specs=[pl.BlockSpec((B, D), lambda l, n: (0, 0)),
                  pl.BlockSpec((1, D, tn), lambda l, n: (l, 0, n)),
                  pl.BlockSpec((1, 1, tn), lambda l, n: (l, 0, n))],
        out_specs=pl.BlockSpec((1, B, tn), lambda l, n: (l, 0, n)),
        out_shape=jax.ShapeDtypeStruct((L, B, N), F32),
        compiler_params=_cparams("parallel", "parallel"),
        name="adaln_modulation",
    )(c, ada_w, ada_b.reshape(L, 1, N))


def _wprep_kernel(w_ref, o_ref):
    tail = N_IN - REF_TAIL
    o_ref[0, :, 0:tail] = w_ref[0, :, REF_TAIL:N_IN].astype(BF16)
    o_ref[0, :, tail:tail + REF_ALIGNED] = w_ref[0, :, 0:REF_ALIGNED].astype(BF16)
    kw = w_ref[0, :, REF_ALIGNED:REF_ALIGNED + LANES]
    lane = lax.broadcasted_iota(jnp.int32, kw.shape, 1)
    o_ref[0, :, COL_IX_KW:N_PROJ] = jnp.where(lane < REF_TAIL - REF_ALIGNED, kw, 0.0).astype(BF16)


def _permute_w_in(w_in):
    L, D, N = w_in.shape
    tr = 128
    return pl.pallas_call(
        _wprep_kernel,
        grid=(L, D // tr),
        in_specs=[pl.BlockSpec((1, tr, N), lambda l, r: (l, r, 0))],
        out_specs=pl.BlockSpec((1, tr, N_PROJ), lambda l, r: (l, r, 0)),
        out_shape=jax.ShapeDtypeStruct((L, D, N_PROJ), BF16),
        compiler_params=_cparams("parallel", "parallel"),
        name="w_in_prepare",
    )(w_in)


def _proj_kernel(x_ref, g_ref, shift_ref, scale_ref, w_ref, o_ref, f_ref, u_ref, h_ref):
    j = pl.program_id(1)

    @pl.when(j == 0)
    def _():
        x = x_ref[...]
        r = lax.rsqrt(jnp.mean(x * x, axis=-1, keepdims=True) + EPS)
        h = (x * r) * g_ref[...] * (1.0 + scale_ref[0]) + shift_ref[0]
        h_ref[...] = h.astype(BF16)

    acc = jnp.dot(h_ref[...], w_ref[0], preferred_element_type=F32)
    o_ref[...] = acc.astype(BF16)

    @pl.when(j == COL_HG_F // PROJ_TN)
    def _():
        f_ref[...] = acc[:, 0:BRANCH_WIDTH]

    @pl.when(j == COL_S5_U // PROJ_TN)
    def _():
        u_ref[...] = acc[:, 0:BRANCH_WIDTH]


def _projection(x2, norm_g, shift, scale, w_all, layer, seq):
    M, D = x2.shape
    tm = min(1024, seq)
    per_batch = seq // tm
    return pl.pallas_call(
        _proj_kernel,
        grid=(M // tm, N_PROJ // PROJ_TN),
        in_specs=[pl.BlockSpec((tm, D), lambda i, j: (i, 0)),
                  pl.BlockSpec((1, D), lambda i, j: (0, 0)),
                  pl.BlockSpec((1, 1, D), lambda i, j: (i // per_batch, 0, 0)),
                  pl.BlockSpec((1, 1, D), lambda i, j: (i // per_batch, 0, 0)),
                  pl.BlockSpec((1, D, PROJ_TN), lambda i, j: (layer, 0, j))],
        out_specs=[pl.BlockSpec((tm, PROJ_TN), lambda i, j: (i, j)),
                   pl.BlockSpec((tm, BRANCH_WIDTH), lambda i, j: (i, 0)),
                   pl.BlockSpec((tm, BRANCH_WIDTH), lambda i, j: (i, 0))],
        out_shape=[jax.ShapeDtypeStruct((M, N_PROJ), BF16),
                   jax.ShapeDtypeStruct((M, BRANCH_WIDTH), F32),
                   jax.ShapeDtypeStruct((M, BRANCH_WIDTH), F32)],
        scratch_shapes=[pltpu.VMEM((tm, D), BF16)],
        compiler_params=_cparams("parallel", "arbitrary"),
        name="norm_in_proj",
    )(x2, norm_g, shift, scale, w_all)


def _split3(x):
    hi = x.astype(BF16)
    r1 = x - hi.astype(F32)
    mid = r1.astype(BF16)
    lo = (r1 - mid.astype(F32)).astype(BF16)
    return hi, mid, lo


def _hgrn_kernel(q_ref, f_ref, i_ref, g_ref, lb_ref, on_ref, o_ref, state_ref, *, rows):
    C, c = HG_CHUNK, HG_SUB
    nsub = C // c

    @pl.when(pl.program_id(2) == 0)
    def _():
        state_ref[...] = jnp.zeros_like(state_ref)

    lb = lb_ref[...]
    row_id = lax.broadcasted_iota(jnp.int32, (C, C), 0)
    col_id = lax.broadcasted_iota(jnp.int32, (C, C), 1)
    tril = jnp.where(col_id <= row_id, 1.0, 0.0).astype(BF16)
    sub_row = lax.broadcasted_iota(jnp.int32, (c, C), 0)
    sub_col = lax.broadcasted_iota(jnp.int32, (c, C), 1)

    for ci in range(rows // C):
        sl = slice(ci * C, (ci + 1) * C)
        q = q_ref[sl, :].astype(F32)
        v16 = i_ref[sl, :]
        fg = lb + (1.0 - lb) * _sigmoid(f_ref[sl, :])
        lf = jnp.log(jnp.maximum(fg, 1e-30))
        k = 1.0 - fg
        hi, mid, lo = _split3(lf)
        b = (jnp.dot(tril, hi, preferred_element_type=F32)
             + jnp.dot(tril, mid, preferred_element_type=F32)
             + jnp.dot(tril, lo, preferred_element_type=F32))
        b_last = b[C - 1:C, :]

        state_t = state_ref[...]
        q0 = (q * jnp.exp(b)).astype(BF16)
        o = lax.dot_general(q0, state_t.astype(BF16), _NT, preferred_element_type=F32)

        row_blocks = []
        for t in range(nsub):
            r0 = t * c
            bt = b[r0:r0 + c, :]
            qt = q[r0:r0 + c, :]
            diag = jnp.zeros((c, C), F32)
            for s in range(c):
                e = jnp.exp(jnp.minimum(bt - b[r0 + s:r0 + s + 1, :], 0.0))
                col = jnp.sum(qt * e * k[r0 + s:r0 + s + 1, :], axis=-1, keepdims=True)
                diag = jnp.where(sub_col == r0 + s, col, diag)
            diag = jnp.where(sub_col - r0 <= sub_row, diag, 0.0)
            if t == 0:
                row_blocks.append(diag)
                continue
            r_t = b[r0 - 1:r0, :]
            q_t = (qt * jnp.exp(bt - r_t)).astype(BF16)
            k_t = (k * jnp.exp(jnp.minimum(r_t - b, 0.0))).astype(BF16)
            off = lax.dot_general(q_t, k_t, _NT, preferred_element_type=F32)
            row_blocks.append(jnp.where(sub_col < r0, off, diag))
        scores = jnp.concatenate(row_blocks, axis=0)
        o = o + jnp.dot(scores.astype(BF16), v16, preferred_element_type=F32)

        k_dec = (k * jnp.exp(b_last - b)).astype(BF16)
        state_ref[...] = (state_t * jnp.exp(b_last)
                          + lax.dot_general(v16, k_dec, _TN, preferred_element_type=F32))

        r = lax.rsqrt(jnp.mean(o * o, axis=-1, keepdims=True) + EPS)
        o_ref[sl, :] = ((o * r) * on_ref[...] * _silu(g_ref[sl, :].astype(F32))).astype(BF16)


def _hgrn(proj, hgf, lb, onorm_g, batch, seq):
    rows = min(512, seq)
    nr = seq // rows
    cb = lambda base: (lambda b, h, r: (b * nr + r, base // HG_DIM + h))
    return pl.pallas_call(
        functools.partial(_hgrn_kernel, rows=rows),
        grid=(batch, HG_HEADS, nr),
        in_specs=[pl.BlockSpec((rows, HG_DIM), cb(COL_HG_Q)),
                  pl.BlockSpec((rows, HG_DIM), cb(0)),
                  pl.BlockSpec((rows, HG_DIM), cb(COL_HG_I)),
                  pl.BlockSpec((rows, HG_DIM), cb(COL_HG_G)),
                  pl.BlockSpec((1, HG_DIM), lambda b, h, r: (0, h)),
                  pl.BlockSpec((1, HG_DIM), lambda b, h, r: (0, 0))],
        out_specs=pl.BlockSpec((rows, HG_DIM), lambda b, h, r: (b * nr + r, h)),
        out_shape=jax.ShapeDtypeStruct((batch * seq, BRANCH_WIDTH), BF16),
        scratch_shapes=[pltpu.VMEM((HG_DIM, HG_DIM), F32)],
        compiler_params=_cparams("parallel", "parallel", "arbitrary"),
        name="hgrn2_mixer",
    )(proj, hgf, proj, proj, lb, onorm_g)


def _rope_tables(positions):
    pos = positions.astype(F32)[..., None]

    def tables(dim, reps):
        inv = ROPE_THETA ** (-jnp.arange(0, dim, 2, dtype=F32) / dim)
        ang = pos * inv
        c, s = jnp.cos(ang), jnp.sin(ang)
        return (jnp.concatenate([c, c] * reps, axis=-1),
                jnp.concatenate([-s, s] * reps, axis=-1))

    return tables(AT_DIM, 1) + tables(IDX_DIM, LANES // IDX_DIM)


def _rope_full(x, cos, sin_signed):
    return x * cos + pltpu.roll(x, AT_DIM // 2, 1) * sin_signed


def _rope_idx(x, cos, sin_signed, first_half):
    h = IDX_DIM // 2
    partner = jnp.where(first_half, pltpu.roll(x, LANES - h, 1), pltpu.roll(x, h, 1))
    return x * cos + partner * sin_signed


def _col_reduce(x, op):
    part = 64
    if x.shape[0] > part:
        x = op(x.reshape(x.shape[0] // part, part, x.shape[1]), axis=0)
    return op(x, axis=0, keepdims=True)


def _dsa_block(width, j, q_ref, g_refs, iq_refs, w_rows, ca, sa, ci, si, qn_ref, o_ref,
               k_s, vt_s, ik_s, key_s, val_s, *, topk, first_half, low_group):
    QB = Q_BLOCK
    kk_s, ikk_s = k_s.at[0:width, :], ik_s.at[0:width, :]
    key_w, val_w = key_s.at[0:width, :], val_s.at[0:width, :]

    isc = jnp.zeros((width, QB), F32)
    per_slab = LANES // IDX_DIM
    for m in range(IDX_HEADS // per_slab):
        iq_ref = iq_refs[m // 2]
        off = (m % 2) * LANES
        xr = _rope_idx(iq_ref[:, off:off + LANES].astype(F32), ci, si, first_half)
        for par in range(per_slab):
            h = m * per_slab + par
            xm = jnp.where(low_group if par == 0 else jnp.logical_not(low_group), xr, 0.0)
            rel = lax.dot_general(ikk_s[...], xm.astype(BF16), _NT, preferred_element_type=F32)
            isc = isc + jnp.maximum(rel, 0.0) * w_rows[IDX_DIM + h:IDX_DIM + h + 1, :]
    qpos = j * QB + lax.broadcasted_iota(jnp.int32, (1, QB), 1)
    kpos = lax.broadcasted_iota(jnp.int32, (width, 1), 0)
    causal = kpos <= qpos
    key_w[...] = jnp.where(causal, isc, NEG_BIG)

    kf = float(topk)

    def as_score(image):
        bits = jnp.where(image < 0, image ^ jnp.int32(0x7FFFFFFF), image)
        return pltpu.bitcast(bits, F32)

    def count_ge(cand):
        return _col_reduce(jnp.where(key_w[...] >= as_score(cand), 1.0, 0.0), jnp.sum)

    zero = jnp.zeros((1, QB), jnp.int32)
    ans0 = jnp.where(count_ge(zero) >= kf, zero, jnp.full((1, QB), -2**31, jnp.int32))

    def bisect(i, ans):
        cand = ans + (jnp.int32(1) << (30 - i))
        return jnp.where(count_ge(cand) >= kf, cand, ans)

    thr = as_score(lax.fori_loop(0, 31, bisect, ans0))

    key = key_w[...]
    gt = key > thr
    eq = key == thr
    need = kf - _col_reduce(jnp.where(gt, 1.0, 0.0), jnp.sum)
    rr = lax.broadcasted_iota(jnp.int32, (LANES, LANES), 0)
    cc = lax.broadcasted_iota(jnp.int32, (LANES, LANES), 1)
    lower = jnp.where(cc <= rr, 1.0, 0.0).astype(BF16)
    seen = jnp.zeros((1, QB), F32)
    for cidx in range(width // LANES):
        sl = slice(cidx * LANES, (cidx + 1) * LANES)
        eq_c = jnp.where(eq[sl, :], 1.0, 0.0)
        rank = jnp.dot(lower, eq_c.astype(BF16), preferred_element_type=F32) + seen
        take = jnp.where(gt[sl, :], 1.0, jnp.where(rank <= need, eq_c, 0.0))
        val_w[sl, :] = jnp.where(causal[sl, :], take, 0.0)
        seen = rank[LANES - 1:LANES, :]

    scale = 1.0 / math.sqrt(AT_DIM)
    valid = val_w[...] > 0.0
    for h in range(AT_HEADS):
        hs = slice(h * AT_DIM, (h + 1) * AT_DIM)
        qh = q_ref[:, hs].astype(F32)
        r = lax.rsqrt(jnp.mean(qh * qh, axis=-1, keepdims=True) + EPS)
        qh = _rope_full((qh * r) * qn_ref[...], ca, sa) * scale
        logits = lax.dot_general(kk_s[...], qh.astype(BF16), _NT, preferred_element_type=F32)
        logits = jnp.where(valid, logits, NEG_BIG)
        p = jnp.exp(logits - _col_reduce(logits, jnp.max))
        denom = _col_reduce(p, jnp.sum)
        oh_t = jnp.dot(vt_s[:, 0:width], p.astype(BF16), preferred_element_type=F32) / denom
        g_ref = g_refs[h // 2]
        goff = (h % 2) * AT_DIM
        o_ref[:, hs] = (oh_t.T * _silu(g_ref[:, goff:goff + AT_DIM].astype(F32))).astype(BF16)


def _dsa_kernel(q_ref, g0_ref, g1_ref, iq0_ref, iq1_ref, k_ref, v_ref, ikw_ref,
                ca_ref, sa_ref, ci_ref, si_ref, qn_ref, kn_ref, o_ref,
                k_s, vt_s, ik_s, key_s, val_s, *, seq, topk):
    j = pl.program_id(1)
    QB = Q_BLOCK
    lane = lax.broadcasted_iota(jnp.int32, (1, LANES), 1)
    first_half = (lane % IDX_DIM) < (IDX_DIM // 2)
    low_group = lane < IDX_DIM

    @pl.when(j == 0)
    def _():
        kk = k_ref[...].astype(F32)
        r = lax.rsqrt(jnp.mean(kk * kk, axis=-1, keepdims=True) + EPS)
        kk = (kk * r) * kn_ref[...]
        k_s[...] = _rope_full(kk, ca_ref[0], sa_ref[0]).astype(BF16)
        vt_s[...] = v_ref[...].astype(F32).T.astype(BF16)
        ik = _rope_idx(ikw_ref[...].astype(F32), ci_ref[0], si_ref[0], first_half)
        ik_s[...] = jnp.where(low_group, ik, pltpu.roll(ik, IDX_DIM, 1)).astype(BF16)

    rows = pl.ds(pl.multiple_of(j * QB, QB), QB)
    ca, sa = ca_ref[0, rows, :], sa_ref[0, rows, :]
    ci, si = ci_ref[0, rows, :], si_ref[0, rows, :]
    w_rows = ikw_ref[rows, :].astype(F32).T

    group = min(KEY_GROUP, seq)
    per_group = group // QB
    for n in range(1, seq // group + 1):
        @pl.when(j // per_group + 1 == n)
        def _(n=n):
            _dsa_block(n * group, j, q_ref, (g0_ref, g1_ref), (iq0_ref, iq1_ref), w_rows,
                       ca, sa, ci, si, qn_ref, o_ref, k_s, vt_s, ik_s, key_s, val_s,
                       topk=topk, first_half=first_half, low_group=low_group)


def _dsa(proj, tables, qn_g, kn_g, batch, seq):
    nb = seq // Q_BLOCK
    topk = min(TOPK_MAX, seq // 4)
    ca, sa, ci, si = tables
    half = BRANCH_WIDTH // 2
    qrow = lambda width, col: pl.BlockSpec((Q_BLOCK, width), lambda b, j: (b * nb + j, col // width))
    krow = lambda col: pl.BlockSpec((seq, LANES), lambda b, j: (b, col // LANES))
    tab = pl.BlockSpec((1, seq, LANES), lambda b, j: (b, 0, 0))
    vec = pl.BlockSpec((1, LANES), lambda b, j: (0, 0))
    return pl.pallas_call(
        functools.partial(_dsa_kernel, seq=seq, topk=topk),
        grid=(batch, nb),
        in_specs=[qrow(BRANCH_WIDTH, COL_AT_Q),
                  qrow(half, COL_AT_G), qrow(half, COL_AT_G + half),
                  qrow(half, COL_IX_Q), qrow(half, COL_IX_Q + half),
                  krow(COL_AT_K), krow(COL_AT_V), krow(COL_IX_KW),
                  tab, tab, tab, tab, vec, vec],
        out_specs=pl.BlockSpec((Q_BLOCK, BRANCH_WIDTH), lambda b, j: (b * nb + j, 0)),
        out_shape=jax.ShapeDtypeStruct((batch * seq, BRANCH_WIDTH), BF16),
        scratch_shapes=[pltpu.VMEM((seq, LANES), BF16), pltpu.VMEM((LANES, seq), BF16),
                        pltpu.VMEM((seq, LANES), BF16),
                        pltpu.VMEM((seq, Q_BLOCK), F32), pltpu.VMEM((seq, Q_BLOCK), F32)],
        compiler_params=_cparams("parallel", "arbitrary"),
        name="dsa_mixer",
    )(proj, proj, proj, proj, proj, proj, proj, proj, ca, sa, ci, si, qn_g, kn_g)


def _s5prep_kernel(are_ref, aim_ref, ldt_ref, btr_ref, bti_ref, cr_ref, ci_ref,
                   t_ref, min_ref, nout_ref, al_ref):
    Lb, G16 = S5_BLOCK, S5_GROUP
    a_re, a_im = are_ref[0], aim_ref[0]
    dt = jnp.exp(ldt_ref[0])
    mag = jnp.exp(a_re * dt)
    ang = a_im * dt
    ab_r, ab_i = mag * jnp.cos(ang), mag * jnp.sin(ang)
    nr, ni = ab_r - 1.0, ab_i
    den = a_re * a_re + a_im * a_im
    fr = (nr * a_re + ni * a_im) / den
    fi = (ni * a_re - nr * a_im) / den
    bt_r, bt_i = btr_ref[0], bti_ref[0]
    bb_r = fr * bt_r - fi * bt_i
    bb_i = fr * bt_i + fi * bt_r
    c_r, c_i = cr_ref[0], ci_ref[0]

    pw_r, pw_i = [jnp.ones_like(ab_r)], [jnp.zeros_like(ab_r)]
    for _ in range(Lb):
        pr, pi = pw_r[-1], pw_i[-1]
        pw_r.append(pr * ab_r - pi * ab_i)
        pw_i.append(pr * ab_i + pi * ab_r)
    tab_r = jnp.concatenate(pw_r[:Lb], axis=0)
    tab_i = jnp.concatenate(pw_i[:Lb], axis=0)

    g1_r = jnp.concatenate([c_r * bb_r[i:i + 1] - c_i * bb_i[i:i + 1] for i in range(G16)], axis=0)
    g1_i = jnp.concatenate([c_r * bb_i[i:i + 1] + c_i * bb_r[i:i + 1] for i in range(G16)], axis=0)

    def shifted(tab, s):
        if s == 0:
            return tab
        return jnp.concatenate([jnp.zeros((s, tab.shape[1]), F32), tab[:Lb - s]], axis=0)

    w_r = jnp.concatenate([shifted(tab_r, s) for s in range(Lb)], axis=0)
    w_i = jnp.concatenate([shifted(tab_i, s) for s in range(Lb)], axis=0)
    hp = lax.Precision.HIGHEST
    t_ref[0] = (lax.dot_general(g1_r, w_r, _NT, precision=hp, preferred_element_type=F32)
                - lax.dot_general(g1_i, w_i, _NT, precision=hp, preferred_element_type=F32))

    m_r = jnp.concatenate([pw_r[Lb - 1 - s] * bb_r - pw_i[Lb - 1 - s] * bb_i for s in range(Lb)], axis=0)
    m_i = jnp.concatenate([pw_r[Lb - 1 - s] * bb_i + pw_i[Lb - 1 - s] * bb_r for s in range(Lb)], axis=0)
    min_ref[0] = jnp.concatenate([m_r, m_i, m_i, m_r], axis=1)
    n_r = jnp.concatenate([c_r * pw_r[t + 1] - c_i * pw_i[t + 1] for t in range(Lb)], axis=0)
    n_i = jnp.concatenate([c_r * pw_i[t + 1] + c_i * pw_r[t + 1] for t in range(Lb)], axis=0)
    nout_ref[0] = jnp.concatenate([n_r, -n_i], axis=1)
    al_r, al_i = pw_r[Lb], pw_i[Lb]
    al_ref[0] = jnp.concatenate([jnp.concatenate([al_r, al_r], axis=1),
                                 jnp.concatenate([-al_i, al_i], axis=1),
                                 jnp.zeros((6, 2 * S5_STATE), F32)], axis=0)


def _s5_prepare(a_re, a_im, log_dt, b_re, b_im, c_re, c_im):
    G, P, K = S5_GROUPS, S5_STATE, S5_BLOCK * S5_GROUP
    g3 = lambda shape: pl.BlockSpec((1,) + shape, lambda g: (g, 0, 0))
    t_flat, m_in, n_out, al = pl.pallas_call(
        _s5prep_kernel,
        grid=(G,),
        in_specs=[g3((1, P)), g3((1, P)), g3((1, 1)), g3((S5_GROUP, P)), g3((S5_GROUP, P)),
                  g3((S5_GROUP, P)), g3((S5_GROUP, P))],
        out_specs=[g3((K, K)), g3((K, 4 * P)), g3((K, 2 * P)), g3((8, 2 * P))],
        out_shape=[jax.ShapeDtypeStruct((G, K, K), F32), jax.ShapeDtypeStruct((G, K, 4 * P), F32),
                   jax.ShapeDtypeStruct((G, K, 2 * P), F32), jax.ShapeDtypeStruct((G, 8, 2 * P), F32)],
        compiler_params=_cparams("parallel"),
        name="s5_prepare",
    )(a_re.reshape(G, 1, P), a_im.reshape(G, 1, P), log_dt.reshape(G, 1, 1),
      jnp.swapaxes(b_re, 1, 2), jnp.swapaxes(b_im, 1, 2), c_re, c_im)
    t_mat = t_flat.reshape(G, S5_GROUP, S5_GROUP, S5_BLOCK, S5_BLOCK).transpose(0, 3, 1, 4, 2)
    t_mat = t_mat.reshape(G, K, K)
    idx = jnp.asarray(_s5_slot_index())
    rows = idx[:, :, None]
    t_mat = jnp.take_along_axis(jnp.take_along_axis(t_mat, rows, axis=1), idx[:, None, :], axis=2)
    m_in = jnp.take_along_axis(m_in, rows, axis=1)
    n_out = jnp.take_along_axis(n_out, rows, axis=1)
    return t_mat.astype(BF16), m_in.astype(BF16), n_out.astype(BF16), al


def _s5_slot(step, g):
    half = LANES // S5_GROUP
    return (step // half) * half + (g % half + step % half) % half


def _s5_slot_index():
    idx = np.zeros((S5_GROUPS, S5_BLOCK * S5_GROUP), np.int32)
    for g in range(S5_GROUPS):
        for step in range(S5_BLOCK):
            for ch in range(S5_GROUP):
                idx[g, _s5_slot(step, g) * S5_GROUP + ch] = step * S5_GROUP + ch
    return idx


def _s5_in_kernel(*refs, nblk):
    u_refs, o_ref = refs[:-1], refs[-1]
    half = LANES // S5_GROUP
    for k, u_ref in enumerate(u_refs):
        for step in range(S5_BLOCK):
            slab = u_ref[pl.ds(step, nblk, stride=S5_BLOCK), :]
            shift = (step % half) * S5_GROUP
            if shift:
                slab = pltpu.roll(slab, shift, 1)
            slab = slab.astype(BF16)
            for gi in range(half):
                g = k * half + gi
                lo = (_s5_slot(step, g) % half) * S5_GROUP
                base = (step // half) * LANES
                o_ref[g, :, base + lo:base + lo + S5_GROUP] = slab[:, lo:lo + S5_GROUP]


def _s5_out_kernel(*refs, nblk):
    y_ref, o_refs = refs[0], refs[1:]
    half = LANES // S5_GROUP
    slot_of_lane = lax.broadcasted_iota(jnp.int32, (1, LANES), 1) // S5_GROUP
    for k, o_ref in enumerate(o_refs):
        for step in range(S5_BLOCK):
            base = (step // half) * LANES
            slab = None
            for gi in range(half):
                g = k * half + gi
                src = y_ref[g, :, base:base + LANES]
                slab = src if slab is None else jnp.where(slot_of_lane == _s5_slot(step, g) % half, src, slab)
            slab = slab.astype(F32)
            shift = (step % half) * S5_GROUP
            if shift:
                slab = pltpu.roll(slab, LANES - shift, 1)
            o_ref[pl.ds(step, nblk, stride=S5_BLOCK), :] = slab


def _s5_kernel(u_ref, t_ref, min_ref, nout_ref, al_ref, y_ref, inj_s, injx_s, xp_s, *, batch, nblk):
    P2 = 2 * S5_STATE
    u = u_ref[0]
    inj = jnp.dot(u, min_ref[0], preferred_element_type=F32)
    inj_s[...] = inj[:, 0:P2]
    injx_s[...] = inj[:, P2:2 * P2]
    a_same = al_ref[0, 0:1, :]
    a_cross = al_ref[0, 1:2, :]

    def step(jb, carry):
        x, xs = carry
        rows = pl.ds(jb, batch, stride=nblk)
        xp_s[rows, :] = x
        return (x * a_same + xs * a_cross + inj_s[rows, :],
                xs * a_same - x * a_cross + injx_s[rows, :])

    z = jnp.zeros((batch, P2), F32)
    lax.fori_loop(0, nblk, step, (z, z))
    y = jnp.dot(u, t_ref[0], preferred_element_type=F32)
    y = y + lax.dot_general(xp_s[...].astype(BF16), nout_ref[0], _NT, preferred_element_type=F32)
    y_ref[0] = y.astype(BF16)


def _s5_scan(u32, ops, batch, seq):
    t_mat, m_in, n_out, al = ops
    G, P, K = S5_GROUPS, S5_STATE, S5_BLOCK * S5_GROUP
    nblk = seq // S5_BLOCK
    R = nblk * batch
    nslab = BRANCH_WIDTH // LANES
    slab = lambda k: pl.BlockSpec((seq, LANES), lambda b, k=k: (b, k))
    grouped = pl.BlockSpec((G, nblk, K), lambda b: (0, b, 0))
    u = pl.pallas_call(
        functools.partial(_s5_in_kernel, nblk=nblk),
        grid=(batch,),
        in_specs=[slab(k) for k in range(nslab)],
        out_specs=grouped,
        out_shape=jax.ShapeDtypeStruct((G, R, K), BF16),
        compiler_params=_cparams("parallel"),
        name="s5_relayout_in",
    )(*([u32] * nslab))
    g3 = lambda shape: pl.BlockSpec((1,) + shape, lambda g: (g, 0, 0))
    y = pl.pallas_call(
        functools.partial(_s5_kernel, batch=batch, nblk=nblk),
        grid=(G,),
        in_specs=[g3((R, K)), g3((K, K)), g3((K, 4 * P)), g3((K, 2 * P)), g3((8, 2 * P))],
        out_specs=g3((R, K)),
        out_shape=jax.ShapeDtypeStruct((G, R, K), BF16),
        scratch_shapes=[pltpu.VMEM((R, 2 * P), F32), pltpu.VMEM((R, 2 * P), F32),
                        pltpu.VMEM((R, 2 * P), F32)],
        compiler_params=_cparams("parallel"),
        name="s5_scan",
    )(u, t_mat, m_in, n_out, al)
    return pl.pallas_call(
        functools.partial(_s5_out_kernel, nblk=nblk),
        grid=(batch,),
        in_specs=[grouped],
        out_specs=[pl.BlockSpec((seq, LANES), lambda b: (b, 0)) for _ in range(nslab)],
        out_shape=[jax.ShapeDtypeStruct((batch * seq, LANES), F32) for _ in range(nslab)],
        compiler_params=_cparams("parallel"),
        name="s5_relayout_out",
    )(y)


def _merge_kernel(x_ref, gate_ref, ya_ref, yb_ref, y50_ref, y51_ref, y52_ref, y53_ref, u_ref, sg_ref,
                  m0_ref, m1_ref, m2_ref, dsk_ref, gw_ref, gb_ref, wb_ref, wo_ref, o_ref):
    y5 = jnp.concatenate([y50_ref[...], y51_ref[...], y52_ref[...], y53_ref[...]], axis=1)
    y = y5 + dsk_ref[...] * u_ref[...].astype(F32)
    y = jax.nn.gelu(y)
    glu = jnp.dot(y.astype(BF16), gw_ref[...], preferred_element_type=F32) + gb_ref[...]
    yc = (y * _sigmoid(glu) * _silu(sg_ref[...].astype(F32))).astype(BF16)
    merged = jnp.zeros(x_ref.shape, F32)
    for n, (yn, m_ref) in enumerate(((ya_ref[...], m0_ref), (yb_ref[...], m1_ref), (yc, m2_ref))):
        yd = jnp.dot(yn, wb_ref[n], preferred_element_type=F32)
        merged = merged + _sigmoid(m_ref[...].astype(F32)) * yd
    o_ref[...] = x_ref[...] + gate_ref[0] * jnp.dot(merged.astype(BF16), wo_ref[...],
                                                    preferred_element_type=F32)


def _merge(x2, gate, ya, yb, y5, proj, d_skip, glu_w, glu_b, w_branch, w_out, seq):
    M, D = x2.shape
    W = BRANCH_WIDTH
    tm = min(512, seq)
    per_batch = seq // tm
    row = lambda cols, colblk: pl.BlockSpec((tm, cols), lambda i: (i, colblk))
    full = lambda shape: pl.BlockSpec(shape, lambda i: (0,) * len(shape))
    return pl.pallas_call(
        _merge_kernel,
        grid=(M // tm,),
        in_specs=[row(D, 0),
                  pl.BlockSpec((1, 1, D), lambda i: (i // per_batch, 0, 0)),
                  row(W, 0), row(W, 0),
                  row(LANES, 0), row(LANES, 0), row(LANES, 0), row(LANES, 0),
                  row(W, COL_S5_U // W), row(W, COL_S5_G // W),
                  row(D, COL_MERGE // D), row(D, COL_MERGE // D + 1), row(D, COL_MERGE // D + 2),
                  full((1, W)), full((W, W)), full((1, W)), full((N_BRANCH, W, D)), full((D, D))],
        out_specs=row(D, 0),
        out_shape=jax.ShapeDtypeStruct((M, D), F32),
        compiler_params=_cparams("parallel"),
        name="merge_out",
    )(x2, gate, ya, yb, *y5, proj, proj, proj, proj, proj, d_skip, glu_w, glu_b, w_branch, w_out)


def kernel(x, c, positions, ada_w, ada_b, norm_g, w_in, hg_lb_logits, hg_onorm_g, at_qnorm_g,
           at_knorm_g, s5_a_re, s5_a_im, s5_log_dt, s5_b_re, s5_b_im, s5_c_re, s5_c_im, s5_d,
           s5_glu_w, s5_glu_b, w_branch, w_out):
    B, S, D = x.shape
    L = ada_w.shape[0]
    lb_all = _lower_bounds(hg_lb_logits.astype(F32))
    mod = _modulation(c, ada_w, ada_b)
    w_all = _permute_w_in(w_in)
    tables = _rope_tables(positions)
    x2 = x.reshape(B * S, D)
    for l in range(L):
        shift = mod[l, :, None, 0:D]
        scale = mod[l, :, None, D:2 * D]
        gate = mod[l, :, None, 2 * D:3 * D]
        proj, hgf, u32 = _projection(x2, norm_g[l][None, :], shift, scale, w_all, l, S)
        ya = _hgrn(proj, hgf, lb_all[l][None, :], hg_onorm_g[l][None, :], B, S)
        yb = _dsa(proj, tables, at_qnorm_g[l][None, :], at_knorm_g[l][None, :], B, S)
        ops = _s5_prepare(s5_a_re[l], s5_a_im[l], s5_log_dt[l], s5_b_re[l], s5_b_im[l],
                          s5_c_re[l], s5_c_im[l])
        y5 = _s5_scan(u32, ops, B, S)
        x2 = _merge(x2, gate, ya, yb, y5, proj, s5_d[l][None, :], s5_glu_w[l].astype(BF16),
                    s5_glu_b[l][None, :], w_branch[l].astype(BF16), w_out[l].astype(BF16), S)
    return x2.reshape(B, S, D)
```

```python
import functools
import math

import jax
import jax.numpy as jnp
import numpy as np
from jax import lax
from jax.experimental import pallas as pl
from jax.experimental.pallas import tpu as pltpu

F32 = jnp.float32
BF16 = jnp.bfloat16

D_MODEL = 1024
DEPTH = 4
BRANCH_WIDTH = 512
N_BRANCH = 3
EPS = 1e-6
NEG_BIG = -1e30
HG_HEADS = 4
HG_DIM = 128
AT_HEADS = 4
AT_DIM = 128
IDX_HEADS = 8
IDX_DIM = 64
TOPK_MAX = 256
Q_BLOCK = 128
ROPE_THETA = 10000.0
S5_GROUP = 16
S5_GROUPS = BRANCH_WIDTH // S5_GROUP
S5_STATE = 64
N_IN = 8008

LANES = 128
HG_CHUNK = 128
HG_SUB = 16
S5_BLOCK = 16
KEY_GROUP = 256
VMEM_LIMIT = 52 * 1024 * 1024

REF_ALIGNED = 3840
REF_TAIL = 3912
COL_S5_U = 0
COL_S5_G = 512
COL_MERGE = 1024
COL_HG_Q = 4096
COL_HG_F = 4608
COL_HG_I = 5120
COL_HG_G = 5632
COL_AT_Q = 6144
COL_AT_K = 6656
COL_AT_V = 6784
COL_AT_G = 6912
COL_IX_Q = 7424
COL_IX_KW = 7936
N_PROJ = 8192
PROJ_TN = 2048
assert COL_HG_F % PROJ_TN + BRANCH_WIDTH <= PROJ_TN and COL_S5_U % PROJ_TN + BRANCH_WIDTH <= PROJ_TN

_NT = (((1,), (1,)), ((), ()))
_TN = (((0,), (0,)), ((), ()))


def _cparams(*sem):
    return pltpu.CompilerParams(dimension_semantics=sem, vmem_limit_bytes=VMEM_LIMIT)


def _sigmoid(x):
    return 1.0 / (1.0 + jnp.exp(-x))


def _silu(x):
    return x * _sigmoid(x)


def _lb_kernel(z_ref, o_ref):
    z = z_ref[...]
    e = jnp.exp(z - jnp.max(z, axis=0, keepdims=True))
    p = e / jnp.sum(e, axis=0, keepdims=True)
    acc = jnp.zeros_like(p[0:1])
    for l in range(z.shape[0]):
        acc = acc + p[l:l + 1]
        o_ref[l:l + 1, :] = acc - p[0:1]


def _lower_bounds(logits):
    return pl.pallas_call(
        _lb_kernel, out_shape=jax.ShapeDtypeStruct(logits.shape, F32), name="hg_lower_bounds",
    )(logits)


def _mod_kernel(c_ref, w_ref, b_ref, o_ref):
    c = c_ref[...]
    o_ref[0] = jnp.dot(_silu(c), w_ref[0], precision=lax.Precision.HIGHEST,
                       preferred_element_type=F32) + b_ref[0]


def _modulation(c, ada_w, ada_b):
    L, D, N = ada_w.shape
    B = c.shape[0]
    tn = 512
    return pl.pallas_call(
        _mod_kernel,
        grid=(L, N // tn),
        in_specs=[pl.BlockSpec((B, D), lambda l, n: (0, 0)),
                  pl.BlockSpec((1, D, tn), lambda l, n: (l, 0, n)),
                  pl.BlockSpec((1, 1, tn), lambda l, n: (l, 0, n))],
        out_specs=pl.BlockSpec((1, B, tn), lambda l, n: (l, 0, n)),
        out_shape=jax.ShapeDtypeStruct((L, B, N), F32),
        compiler_params=_cparams("parallel", "parallel"),
        name="adaln_modulation",
    )(c, ada_w, ada_b.reshape(L, 1, N))


def _wprep_kernel(w_ref, o_ref):
    tail = N_IN - REF_TAIL
    o_ref[0, :, 0:tail] = w_ref[0, :, REF_TAIL:N_IN].astype(BF16)
    o_ref[0, :, tail:tail + REF_ALIGNED] = w_ref[0, :, 0:REF_ALIGNED].astype(BF16)
    kw = w_ref[0, :, REF_ALIGNED:REF_ALIGNED + LANES]
    lane = lax.broadcasted_iota(jnp.int32, kw.shape, 1)
    o_ref[0, :, COL_IX_KW:COL_IX_KW + LANES] = jnp.where(lane < REF_TAIL - REF_ALIGNED, kw, 0.0).astype(BF16)
    o_ref[0, :, COL_IX_KW + LANES:N_PROJ] = jnp.zeros((kw.shape[0], N_PROJ - COL_IX_KW - LANES), BF16)


def _permute_w_in(w_in):
    L, D, N = w_in.shape
    tr = 128
    return pl.pallas_call(
        _wprep_kernel,
        grid=(L, D // tr),
        in_specs=[pl.BlockSpec((1, tr, N), lambda l, r: (l, r, 0))],
        out_specs=pl.BlockSpec((1, tr, N_PROJ), lambda l, r: (l, r, 0)),
        out_shape=jax.ShapeDtypeStruct((L, D, N_PROJ), BF16),
        compiler_params=_cparams("parallel", "parallel"),
        name="w_in_prepare",
    )(w_in)


def _proj_kernel(x_ref, g_ref, shift_ref, scale_ref, w_ref, o_ref, f_ref, u_ref, h_ref):
    j = pl.program_id(1)

    @pl.when(j == 0)
    def _():
        x = x_ref[...]
        r = lax.rsqrt(jnp.mean(x * x, axis=-1, keepdims=True) + EPS)
        h = (x * r) * g_ref[...] * (1.0 + scale_ref[0]) + shift_ref[0]
        h_ref[...] = h.astype(BF16)

    acc = jnp.dot(h_ref[...], w_ref[0], preferred_element_type=F32)
    o_ref[...] = acc.astype(BF16)

    @pl.when(j == COL_HG_F // PROJ_TN)
    def _():
        off = COL_HG_F % PROJ_TN
        f_ref[...] = acc[:, off:off + BRANCH_WIDTH]

    @pl.when(j == COL_S5_U // PROJ_TN)
    def _():
        off = COL_S5_U % PROJ_TN
        u_ref[...] = acc[:, off:off + BRANCH_WIDTH]


def _projection(x2, norm_g, shift, scale, w_all, layer, seq):
    M, D = x2.shape
    tm = min(1024, seq)
    per_batch = seq // tm
    return pl.pallas_call(
        _proj_kernel,
        grid=(M // tm, N_PROJ // PROJ_TN),
        in_specs=[pl.BlockSpec((tm, D), lambda i, j: (i, 0)),
                  pl.BlockSpec((1, D), lambda i, j: (0, 0)),
                  pl.BlockSpec((1, 1, D), lambda i, j: (i // per_batch, 0, 0)),
                  pl.BlockSpec((1, 1, D), lambda i, j: (i // per_batch, 0, 0)),
                  pl.BlockSpec((1, D, PROJ_TN), lambda i, j: (layer, 0, j))],
        out_specs=[pl.BlockSpec((tm, PROJ_TN), lambda i, j: (i, j)),
                   pl.BlockSpec((tm, BRANCH_WIDTH), lambda i, j: (i, 0)),
                   pl.BlockSpec((tm, BRANCH_WIDTH), lambda i, j: (i, 0))],
        out_shape=[jax.ShapeDtypeStruct((M, N_PROJ), BF16),
                   jax.ShapeDtypeStruct((M, BRANCH_WIDTH), F32),
                   jax.ShapeDtypeStruct((M, BRANCH_WIDTH), F32)],
        scratch_shapes=[pltpu.VMEM((tm, D), BF16)],
        compiler_params=_cparams("parallel", "arbitrary"),
        name="norm_in_proj",
    )(x2, norm_g, shift, scale, w_all)


def _split3(x):
    hi = x.astype(BF16)
    r1 = x - hi.astype(F32)
    mid = r1.astype(BF16)
    lo = (r1 - mid.astype(F32)).astype(BF16)
    return hi, mid, lo


def _hgrn_kernel(q_ref, f_ref, i_ref, g_ref, lb_ref, on_ref, o_ref, state_ref, *, rows):
    C, c = HG_CHUNK, HG_SUB
    nsub = C // c

    @pl.when(pl.program_id(2) == 0)
    def _():
        state_ref[...] = jnp.zeros_like(state_ref)

    lb = lb_ref[...]
    row_id = lax.broadcasted_iota(jnp.int32, (C, C), 0)
    col_id = lax.broadcasted_iota(jnp.int32, (C, C), 1)
    tril = jnp.where(col_id <= row_id, 1.0, 0.0).astype(BF16)
    sub_row = lax.broadcasted_iota(jnp.int32, (c, C), 0)
    sub_col = lax.broadcasted_iota(jnp.int32, (c, C), 1)

    for ci in range(rows // C):
        sl = slice(ci * C, (ci + 1) * C)
        q = q_ref[sl, :].astype(F32)
        v16 = i_ref[sl, :]
        fg = lb + (1.0 - lb) * _sigmoid(f_ref[sl, :])
        lf = jnp.log(jnp.maximum(fg, 1e-30))
        k = 1.0 - fg
        hi, mid, lo = _split3(lf)
        b = (jnp.dot(tril, hi, preferred_element_type=F32)
             + jnp.dot(tril, mid, preferred_element_type=F32)
             + jnp.dot(tril, lo, preferred_element_type=F32))
        b_last = b[C - 1:C, :]

        state_t = state_ref[...]
        q0 = (q * jnp.exp(b)).astype(BF16)
        o = lax.dot_general(q0, state_t.astype(BF16), _NT, preferred_element_type=F32)

        row_blocks = []
        for t in range(nsub):
            r0 = t * c
            bt = b[r0:r0 + c, :]
            qt = q[r0:r0 + c, :]
            diag = jnp.zeros((c, C), F32)
            for s in range(c):
                e = jnp.exp(jnp.minimum(bt - b[r0 + s:r0 + s + 1, :], 0.0))
                col = jnp.sum(qt * e * k[r0 + s:r0 + s + 1, :], axis=-1, keepdims=True)
                diag = jnp.where(sub_col == r0 + s, col, diag)
            diag = jnp.where(sub_col - r0 <= sub_row, diag, 0.0)
            if t == 0:
                row_blocks.append(diag)
                continue
            r_t = b[r0 - 1:r0, :]
            q_t = (qt * jnp.exp(bt - r_t)).astype(BF16)
            k_t = (k * jnp.exp(jnp.minimum(r_t - b, 0.0))).astype(BF16)
            off = lax.dot_general(q_t, k_t, _NT, preferred_element_type=F32)
            row_blocks.append(jnp.where(sub_col < r0, off, diag))
        scores = jnp.concatenate(row_blocks, axis=0)
        o = o + jnp.dot(scores.astype(BF16), v16, preferred_element_type=F32)

        k_dec = (k * jnp.exp(b_last - b)).astype(BF16)
        state_ref[...] = (state_t * jnp.exp(b_last)
                          + lax.dot_general(v16, k_dec, _TN, preferred_element_type=F32))

        r = lax.rsqrt(jnp.mean(o * o, axis=-1, keepdims=True) + EPS)
        o_ref[sl, :] = ((o * r) * on_ref[...] * _silu(g_ref[sl, :].astype(F32))).astype(BF16)


def _hgrn(proj, hgf, lb, onorm_g, batch, seq):
    rows = min(512, seq)
    nr = seq // rows
    cb = lambda base: (lambda b, h, r: (b * nr + r, base // HG_DIM + h))
    return pl.pallas_call(
        functools.partial(_hgrn_kernel, rows=rows),
        grid=(batch, HG_HEADS, nr),
        in_specs=[pl.BlockSpec((rows, HG_DIM), cb(COL_HG_Q)),
                  pl.BlockSpec((rows, HG_DIM), cb(0)),
                  pl.BlockSpec((rows, HG_DIM), cb(COL_HG_I)),
                  pl.BlockSpec((rows, HG_DIM), cb(COL_HG_G)),
                  pl.BlockSpec((1, HG_DIM), lambda b, h, r: (0, h)),
                  pl.BlockSpec((1, HG_DIM), lambda b, h, r: (0, 0))],
        out_specs=pl.BlockSpec((rows, HG_DIM), lambda b, h, r: (b * nr + r, h)),
        out_shape=jax.ShapeDtypeStruct((batch * seq, BRANCH_WIDTH), BF16),
        scratch_shapes=[pltpu.VMEM((HG_DIM, HG_DIM), F32)],
        compiler_params=_cparams("parallel", "parallel", "arbitrary"),
        name="hgrn2_mixer",
    )(proj, hgf, proj, proj, lb, onorm_g)


def _rope_tables(positions):
    pos = positions.astype(F32)[..., None]

    def tables(dim, reps):
        inv = ROPE_THETA ** (-jnp.arange(0, dim, 2, dtype=F32) / dim)
        ang = pos * inv
        c, s = jnp.cos(ang), jnp.sin(ang)
        return (jnp.concatenate([c, c] * reps, axis=-1),
                jnp.concatenate([-s, s] * reps, axis=-1))

    return tables(AT_DIM, 1) + tables(IDX_DIM, LANES // IDX_DIM)


def _rope_full(x, cos, sin_signed):
    return x * cos + pltpu.roll(x, AT_DIM // 2, 1) * sin_signed


def _rope_idx(x, cos, sin_signed, first_half):
    h = IDX_DIM // 2
    partner = jnp.where(first_half, pltpu.roll(x, LANES - h, 1), pltpu.roll(x, h, 1))
    return x * cos + partner * sin_signed


def _col_reduce(x, op):
    part = 64
    if x.shape[0] > part:
        x = op(x.reshape(x.shape[0] // part, part, x.shape[1]), axis=0)
    return op(x, axis=0, keepdims=True)


def _dsa_block(width, j, q_ref, g_refs, iq_refs, w_rows, ca, sa, ci, si, qn_ref, o_ref,
               k_s, vt_s, ik_s, key_s, val_s, *, topk, first_half, low_group):
    QB = Q_BLOCK
    kk_s, ikk_s = k_s.at[0:width, :], ik_s.at[0:width, :]
    key_w, val_w = key_s.at[0:width, :], val_s.at[0:width, :]

    isc = jnp.zeros((width, QB), F32)
    per_slab = LANES // IDX_DIM
    for m in range(IDX_HEADS // per_slab):
        iq_ref = iq_refs[m // 2]
        off = (m % 2) * LANES
        xr = _rope_idx(iq_ref[:, off:off + LANES].astype(F32), ci, si, first_half)
        for par in range(per_slab):
            h = m * per_slab + par
            xm = jnp.where(low_group if par == 0 else jnp.logical_not(low_group), xr, 0.0)
            rel = lax.dot_general(ikk_s[...], xm.astype(BF16), _NT, preferred_element_type=F32)
            isc = isc + jnp.maximum(rel, 0.0) * w_rows[IDX_DIM + h:IDX_DIM + h + 1, :]
    qpos = j * QB + lax.broadcasted_iota(jnp.int32, (1, QB), 1)
    kpos = lax.broadcasted_iota(jnp.int32, (width, 1), 0)
    causal = kpos <= qpos
    key_w[...] = jnp.where(causal, isc, NEG_BIG)

    kf = float(topk)

    def as_score(image):
        bits = jnp.where(image < 0, image ^ jnp.int32(0x7FFFFFFF), image)
        return pltpu.bitcast(bits, F32)

    def count_ge(cand):
        return _col_reduce(jnp.where(key_w[...] >= as_score(cand), 1.0, 0.0), jnp.sum)

    zero = jnp.zeros((1, QB), jnp.int32)
    ans0 = jnp.where(count_ge(zero) >= kf, zero, jnp.full((1, QB), -2**31, jnp.int32))

    def bisect(i, ans):
        cand = ans + (jnp.int32(1) << (30 - i))
        return jnp.where(count_ge(cand) >= kf, cand, ans)

    thr = as_score(lax.fori_loop(0, 31, bisect, ans0))

    key = key_w[...]
    gt = key > thr
    eq = key == thr
    need = kf - _col_reduce(jnp.where(gt, 1.0, 0.0), jnp.sum)
    rr = lax.broadcasted_iota(jnp.int32, (LANES, LANES), 0)
    cc = lax.broadcasted_iota(jnp.int32, (LANES, LANES), 1)
    lower = jnp.where(cc <= rr, 1.0, 0.0).astype(BF16)
    seen = jnp.zeros((1, QB), F32)
    for cidx in range(width // LANES):
        sl = slice(cidx * LANES, (cidx + 1) * LANES)
        eq_c = jnp.where(eq[sl, :], 1.0, 0.0)
        rank = jnp.dot(lower, eq_c.astype(BF16), preferred_element_type=F32) + seen
        take = jnp.where(gt[sl, :], 1.0, jnp.where(rank <= need, eq_c, 0.0))
        val_w[sl, :] = jnp.where(causal[sl, :], take, 0.0)
        seen = rank[LANES - 1:LANES, :]

    scale = 1.0 / math.sqrt(AT_DIM)
    valid = val_w[...] > 0.0
    for h in range(AT_HEADS):
        hs = slice(h * AT_DIM, (h + 1) * AT_DIM)
        qh = q_ref[:, hs].astype(F32)
        r = lax.rsqrt(jnp.mean(qh * qh, axis=-1, keepdims=True) + EPS)
        qh = _rope_full((qh * r) * qn_ref[...], ca, sa) * scale
        logits = lax.dot_general(kk_s[...], qh.astype(BF16), _NT, preferred_element_type=F32)
        logits = jnp.where(valid, logits, NEG_BIG)
        p = jnp.exp(logits - _col_reduce(logits, jnp.max))
        denom = _col_reduce(p, jnp.sum)
        oh_t = jnp.dot(vt_s[:, 0:width], p.astype(BF16), preferred_element_type=F32) / denom
        g_ref = g_refs[h // 2]
        goff = (h % 2) * AT_DIM
        o_ref[:, hs] = (oh_t.T * _silu(g_ref[:, goff:goff + AT_DIM].astype(F32))).astype(BF16)


def _dsa_kernel(q_ref, g0_ref, g1_ref, iq0_ref, iq1_ref, k_ref, v_ref, ikw_ref,
                ca_ref, sa_ref, ci_ref, si_ref, qn_ref, kn_ref, o_ref,
                k_s, vt_s, ik_s, key_s, val_s, *, seq, topk):
    j = pl.program_id(1)
    QB = Q_BLOCK
    lane = lax.broadcasted_iota(jnp.int32, (1, LANES), 1)
    first_half = (lane % IDX_DIM) < (IDX_DIM // 2)
    low_group = lane < IDX_DIM

    @pl.when(j == 0)
    def _():
        kk = k_ref[...].astype(F32)
        r = lax.rsqrt(jnp.mean(kk * kk, axis=-1, keepdims=True) + EPS)
        kk = (kk * r) * kn_ref[...]
        k_s[...] = _rope_full(kk, ca_ref[0], sa_ref[0]).astype(BF16)
        vt_s[...] = v_ref[...].astype(F32).T.astype(BF16)
        ik = _rope_idx(ikw_ref[...].astype(F32), ci_ref[0], si_ref[0], first_half)
        ik_s[...] = jnp.where(low_group, ik, pltpu.roll(ik, IDX_DIM, 1)).astype(BF16)

    rows = pl.ds(pl.multiple_of(j * QB, QB), QB)
    ca, sa = ca_ref[0, rows, :], sa_ref[0, rows, :]
    ci, si = ci_ref[0, rows, :], si_ref[0, rows, :]
    w_rows = ikw_ref[rows, :].astype(F32).T

    group = min(KEY_GROUP, seq)
    per_group = group // QB
    for n in range(1, seq // group + 1):
        @pl.when(j // per_group + 1 == n)
        def _(n=n):
            _dsa_block(n * group, j, q_ref, (g0_ref, g1_ref), (iq0_ref, iq1_ref), w_rows,
                       ca, sa, ci, si, qn_ref, o_ref, k_s, vt_s, ik_s, key_s, val_s,
                       topk=topk, first_half=first_half, low_group=low_group)


def _dsa(proj, tables, qn_g, kn_g, batch, seq):
    nb = seq // Q_BLOCK
    topk = min(TOPK_MAX, seq // 4)
    ca, sa, ci, si = tables
    half = BRANCH_WIDTH // 2
    qrow = lambda width, col: pl.BlockSpec((Q_BLOCK, width), lambda b, j: (b * nb + j, col // width))
    krow = lambda col: pl.BlockSpec((seq, LANES), lambda b, j: (b, col // LANES))
    tab = pl.BlockSpec((1, seq, LANES), lambda b, j: (b, 0, 0))
    vec = pl.BlockSpec((1, LANES), lambda b, j: (0, 0))
    return pl.pallas_call(
        functools.partial(_dsa_kernel, seq=seq, topk=topk),
        grid=(batch, nb),
        in_specs=[qrow(BRANCH_WIDTH, COL_AT_Q),
                  qrow(half, COL_AT_G), qrow(half, COL_AT_G + half),
                  qrow(half, COL_IX_Q), qrow(half, COL_IX_Q + half),
                  krow(COL_AT_K), krow(COL_AT_V), krow(COL_IX_KW),
                  tab, tab, tab, tab, vec, vec],
        out_specs=pl.BlockSpec((Q_BLOCK, BRANCH_WIDTH), lambda b, j: (b * nb + j, 0)),
        out_shape=jax.ShapeDtypeStruct((batch * seq, BRANCH_WIDTH), BF16),
        scratch_shapes=[pltpu.VMEM((seq, LANES), BF16), pltpu.VMEM((LANES, seq), BF16),
                        pltpu.VMEM((seq, LANES), BF16),
                        pltpu.VMEM((seq, Q_BLOCK), F32), pltpu.VMEM((seq, Q_BLOCK), F32)],
        compiler_params=_cparams("parallel", "arbitrary"),
        name="dsa_mixer",
    )(proj, proj, proj, proj, proj, proj, proj, proj, ca, sa, ci, si, qn_g, kn_g)


def _s5prep_kernel(are_ref, aim_ref, ldt_ref, btr_ref, bti_ref, cr_ref, ci_ref, perm_ref,
                   t_ref, min_ref, nout_ref, al_ref):
    Lb, G16 = S5_BLOCK, S5_GROUP
    hp = lax.Precision.HIGHEST
    a_re, a_im = are_ref[0], aim_ref[0]
    dt = jnp.exp(ldt_ref[0])
    mag = jnp.exp(a_re * dt)
    ang = a_im * dt
    ab_r, ab_i = mag * jnp.cos(ang), mag * jnp.sin(ang)
    nr, ni = ab_r - 1.0, ab_i
    den = a_re * a_re + a_im * a_im
    fr = (nr * a_re + ni * a_im) / den
    fi = (ni * a_re - nr * a_im) / den
    bt_r, bt_i = btr_ref[0], bti_ref[0]
    bb_r = fr * bt_r - fi * bt_i
    bb_i = fr * bt_i + fi * bt_r
    c_r, c_i = cr_ref[0], ci_ref[0]

    pw_r, pw_i = [jnp.ones_like(ab_r)], [jnp.zeros_like(ab_r)]
    for _ in range(Lb):
        pr, pi = pw_r[-1], pw_i[-1]
        pw_r.append(pr * ab_r - pi * ab_i)
        pw_i.append(pr * ab_i + pi * ab_r)

    def readout(first):
        re = jnp.concatenate([c_r * pw_r[first + t] - c_i * pw_i[first + t] for t in range(Lb)], axis=0)
        im = jnp.concatenate([c_r * pw_i[first + t] + c_i * pw_r[first + t] for t in range(Lb)], axis=0)
        return re, im

    wc_r, wc_i = readout(0)
    taps = (lax.dot_general(bb_r, wc_r, _NT, precision=hp, preferred_element_type=F32)
            - lax.dot_general(bb_i, wc_i, _NT, precision=hp, preferred_element_type=F32))
    t_mat = jnp.concatenate(
        [taps] + [jnp.concatenate([jnp.zeros((G16, s * G16), F32), taps[:, :(Lb - s) * G16]], axis=1)
                  for s in range(1, Lb)], axis=0)
    m_r = jnp.concatenate([pw_r[Lb - 1 - s] * bb_r - pw_i[Lb - 1 - s] * bb_i for s in range(Lb)], axis=0)
    m_i = jnp.concatenate([pw_r[Lb - 1 - s] * bb_i + pw_i[Lb - 1 - s] * bb_r for s in range(Lb)], axis=0)
    m_in = jnp.concatenate([m_r, m_i, m_i, m_r], axis=1)
    n_r, n_i = readout(1)
    n_out = jnp.concatenate([n_r, -n_i], axis=1)

    perm = perm_ref[0]
    t_rows = jnp.dot(perm, t_mat.astype(BF16), preferred_element_type=F32).astype(BF16)
    t_ref[0] = lax.dot_general(t_rows, perm, _NT, preferred_element_type=F32).astype(BF16)
    min_ref[0] = jnp.dot(perm, m_in.astype(BF16), preferred_element_type=F32).astype(BF16)
    nout_ref[0] = jnp.dot(perm, n_out.astype(BF16), preferred_element_type=F32).astype(BF16)
    al_r, al_i = pw_r[Lb], pw_i[Lb]
    al_ref[0] = jnp.concatenate([jnp.concatenate([al_r, al_r], axis=1),
                                 jnp.concatenate([-al_i, al_i], axis=1),
                                 jnp.zeros((6, 2 * S5_STATE), F32)], axis=0)


def _s5_prepare(a_re, a_im, log_dt, b_re, b_im, c_re, c_im):
    G, P, K = S5_GROUPS, S5_STATE, S5_BLOCK * S5_GROUP
    half = LANES // S5_GROUP
    g3 = lambda shape: pl.BlockSpec((1,) + shape, lambda g: (g, 0, 0))
    return pl.pallas_call(
        _s5prep_kernel,
        grid=(G,),
        in_specs=[g3((1, P)), g3((1, P)), g3((1, 1)), g3((S5_GROUP, P)), g3((S5_GROUP, P)),
                  g3((S5_GROUP, P)), g3((S5_GROUP, P)),
                  pl.BlockSpec((1, K, K), lambda g: (g % half, 0, 0))],
        out_specs=[g3((K, K)), g3((K, 4 * P)), g3((K, 2 * P)), g3((8, 2 * P))],
        out_shape=[jax.ShapeDtypeStruct((G, K, K), BF16), jax.ShapeDtypeStruct((G, K, 4 * P), BF16),
                   jax.ShapeDtypeStruct((G, K, 2 * P), BF16), jax.ShapeDtypeStruct((G, 8, 2 * P), F32)],
        compiler_params=_cparams("parallel"),
        name="s5_prepare",
    )(a_re.reshape(G, 1, P), a_im.reshape(G, 1, P), log_dt.reshape(G, 1, 1),
      jnp.swapaxes(b_re, 1, 2), jnp.swapaxes(b_im, 1, 2), c_re, c_im, jnp.asarray(_s5_slot_perm(), BF16))


def _s5_slot(step, g):
    half = LANES // S5_GROUP
    return (step // half) * half + (g % half + step % half) % half


def _s5_slot_perm():
    half = LANES // S5_GROUP
    perm = np.zeros((half, S5_BLOCK * S5_GROUP, S5_BLOCK * S5_GROUP), np.float32)
    for g in range(half):
        for step in range(S5_BLOCK):
            for ch in range(S5_GROUP):
                perm[g, _s5_slot(step, g) * S5_GROUP + ch, step * S5_GROUP + ch] = 1.0
    return perm


def _s5_in_kernel(*refs, nblk):
    u_refs, o_ref = refs[:-1], refs[-1]
    half = LANES // S5_GROUP
    for k, u_ref in enumerate(u_refs):
        for step in range(S5_BLOCK):
            slab = u_ref[pl.ds(step, nblk, stride=S5_BLOCK), :]
            shift = (step % half) * S5_GROUP
            if shift:
                slab = pltpu.roll(slab, shift, 1)
            slab = slab.astype(BF16)
            for gi in range(half):
                g = k * half + gi
                lo = (_s5_slot(step, g) % half) * S5_GROUP
                base = (step // half) * LANES
                o_ref[g, :, base + lo:base + lo + S5_GROUP] = slab[:, lo:lo + S5_GROUP]


def _s5_out_kernel(*refs, nblk):
    y_ref, o_refs = refs[0], refs[1:]
    half = LANES // S5_GROUP
    slot_of_lane = lax.broadcasted_iota(jnp.int32, (1, LANES), 1) // S5_GROUP
    for k, o_ref in enumerate(o_refs):
        for step in range(S5_BLOCK):
            base = (step // half) * LANES
            slab = None
            for gi in range(half):
                g = k * half + gi
                src = y_ref[g, :, base:base + LANES]
                slab = src if slab is None else jnp.where(slot_of_lane == _s5_slot(step, g) % half, src, slab)
            slab = slab.astype(F32)
            shift = (step % half) * S5_GROUP
            if shift:
                slab = pltpu.roll(slab, LANES - shift, 1)
            o_ref[pl.ds(step, nblk, stride=S5_BLOCK), :] = slab


def _s5_kernel(u_ref, t_ref, min_ref, nout_ref, al_ref, y_ref, inj_s, injx_s, xp_s, *, batch, nblk):
    P2 = 2 * S5_STATE
    u = u_ref[0]
    inj = jnp.dot(u, min_ref[0], preferred_element_type=F32)
    inj_s[...] = inj[:, 0:P2]
    injx_s[...] = inj[:, P2:2 * P2]
    a_same = al_ref[0, 0:1, :]
    a_cross = al_ref[0, 1:2, :]

    def step(jb, carry):
        x, xs = carry
        rows = pl.ds(jb, batch, stride=nblk)
        xp_s[rows, :] = x
        return (x * a_same + xs * a_cross + inj_s[rows, :],
                xs * a_same - x * a_cross + injx_s[rows, :])

    z = jnp.zeros((batch, P2), F32)
    lax.fori_loop(0, nblk, step, (z, z))
    y = jnp.dot(u, t_ref[0], preferred_element_type=F32)
    y = y + lax.dot_general(xp_s[...].astype(BF16), nout_ref[0], _NT, preferred_element_type=F32)
    y_ref[0] = y.astype(BF16)


def _s5_scan(u32, ops, batch, seq):
    t_mat, m_in, n_out, al = ops
    G, P, K = S5_GROUPS, S5_STATE, S5_BLOCK * S5_GROUP
    nblk = seq // S5_BLOCK
    R = nblk * batch
    nslab = BRANCH_WIDTH // LANES
    slab = lambda k: pl.BlockSpec((seq, LANES), lambda b, k=k: (b, k))
    grouped = pl.BlockSpec((G, nblk, K), lambda b: (0, b, 0))
    u = pl.pallas_call(
        functools.partial(_s5_in_kernel, nblk=nblk),
        grid=(batch,),
        in_specs=[slab(k) for k in range(nslab)],
        out_specs=grouped,
        out_shape=jax.ShapeDtypeStruct((G, R, K), BF16),
        compiler_params=_cparams("parallel"),
        name="s5_relayout_in",
    )(*([u32] * nslab))
    g3 = lambda shape: pl.BlockSpec((1,) + shape, lambda g: (g, 0, 0))
    y = pl.pallas_call(
        functools.partial(_s5_kernel, batch=batch, nblk=nblk),
        grid=(G,),
        in_specs=[g3((R, K)), g3((K, K)), g3((K, 4 * P)), g3((K, 2 * P)), g3((8, 2 * P))],
        out_specs=g3((R, K)),
        out_shape=jax.ShapeDtypeStruct((G, R, K), BF16),
        scratch_shapes=[pltpu.VMEM((R, 2 * P), F32), pltpu.VMEM((R, 2 * P), F32),
                        pltpu.VMEM((R, 2 * P), F32)],
        compiler_params=_cparams("parallel"),
        name="s5_scan",
    )(u, t_mat, m_in, n_out, al)
    return pl.pallas_call(
        functools.partial(_s5_out_kernel, nblk=nblk),
        grid=(batch,),
        in_specs=[grouped],
        out_specs=[pl.BlockSpec((seq, LANES), lambda b: (b, 0)) for _ in range(nslab)],
        out_shape=[jax.ShapeDtypeStruct((batch * seq, LANES), F32) for _ in range(nslab)],
        compiler_params=_cparams("parallel"),
        name="s5_relayout_out",
    )(y)


def _merge_kernel(x_ref, gate_ref, ya_ref, yb_ref, y50_ref, y51_ref, y52_ref, y53_ref, u_ref, sg_ref,
                  m0_ref, m1_ref, m2_ref, dsk_ref, gw_ref, gb_ref, wb_ref, wo_ref, o_ref):
    y5 = jnp.concatenate([y50_ref[...], y51_ref[...], y52_ref[...], y53_ref[...]], axis=1)
    y = y5 + dsk_ref[...] * u_ref[...].astype(F32)
    y = jax.nn.gelu(y)
    glu = jnp.dot(y.astype(BF16), gw_ref[...], preferred_element_type=F32) + gb_ref[...]
    yc = (y * _sigmoid(glu) * _silu(sg_ref[...].astype(F32))).astype(BF16)
    merged = jnp.zeros(x_ref.shape, F32)
    for n, (yn, m_ref) in enumerate(((ya_ref[...], m0_ref), (yb_ref[...], m1_ref), (yc, m2_ref))):
        yd = jnp.dot(yn, wb_ref[n], preferred_element_type=F32)
        merged = merged + _sigmoid(m_ref[...].astype(F32)) * yd
    o_ref[...] = x_ref[...] + gate_ref[0] * jnp.dot(merged.astype(BF16), wo_ref[...],
                                                    preferred_element_type=F32)


def _merge(x2, gate, ya, yb, y5, proj, d_skip, glu_w, glu_b, w_branch, w_out, seq):
    M, D = x2.shape
    W = BRANCH_WIDTH
    tm = min(512, seq)
    per_batch = seq // tm
    row = lambda cols, colblk: pl.BlockSpec((tm, cols), lambda i: (i, colblk))
    full = lambda shape: pl.BlockSpec(shape, lambda i: (0,) * len(shape))
    return pl.pallas_call(
        _merge_kernel,
        grid=(M // tm,),
        in_specs=[row(D, 0),
                  pl.BlockSpec((1, 1, D), lambda i: (i // per_batch, 0, 0)),
                  row(W, 0), row(W, 0),
                  row(LANES, 0), row(LANES, 0), row(LANES, 0), row(LANES, 0),
                  row(W, COL_S5_U // W), row(W, COL_S5_G // W),
                  row(D, COL_MERGE // D), row(D, COL_MERGE // D + 1), row(D, COL_MERGE // D + 2),
                  full((1, W)), full((W, W)), full((1, W)), full((N_BRANCH, W, D)), full((D, D))],
        out_specs=row(D, 0),
        out_shape=jax.ShapeDtypeStruct((M, D), F32),
        compiler_params=_cparams("parallel"),
        name="merge_out",
    )(x2, gate, ya, yb, *y5, proj, proj, proj, proj, proj, d_skip, glu_w, glu_b, w_branch, w_out)


def kernel(x, c, positions, ada_w, ada_b, norm_g, w_in, hg_lb_logits, hg_onorm_g, at_qnorm_g,
           at_knorm_g, s5_a_re, s5_a_im, s5_log_dt, s5_b_re, s5_b_im, s5_c_re, s5_c_im, s5_d,
           s5_glu_w, s5_glu_b, w_branch, w_out):
    B, S, D = x.shape
    L = ada_w.shape[0]
    lb_all = _lower_bounds(hg_lb_logits.astype(F32))
    mod = _modulation(c, ada_w, ada_b)
    w_all = _permute_w_in(w_in)
    tables = _rope_tables(positions)
    x2 = x.reshape(B * S, D)
    for l in range(L):
        shift = mod[l, :, None, 0:D]
        scale = mod[l, :, None, D:2 * D]
        gate = mod[l, :, None, 2 * D:3 * D]
        proj, hgf, u32 = _projection(x2, norm_g[l][None, :], shift, scale, w_all, l, S)
        ya = _hgrn(proj, hgf, lb_all[l][None, :], hg_onorm_g[l][None, :], B, S)
        yb = _dsa(proj, tables, at_qnorm_g[l][None, :], at_knorm_g[l][None, :], B, S)
        ops = _s5_prepare(s5_a_re[l], s5_a_im[l], s5_log_dt[l], s5_b_re[l], s5_b_im[l],
                          s5_c_re[l], s5_c_im[l])
        y5 = _s5_scan(u32, ops, B, S)
        x2 = _merge(x2, gate, ya, yb, y5, proj, s5_d[l][None, :], s5_glu_w[l].astype(BF16),
                    s5_glu_b[l][None, :], w_branch[l].astype(BF16), w_out[l].astype(BF16), S)
    return x2.reshape(B, S, D)
```

```python
import functools
import math

import jax
import jax.numpy as jnp
import numpy as np
from jax import lax
from jax.experimental import pallas as pl
from jax.experimental.pallas import tpu as pltpu

F32 = jnp.float32
BF16 = jnp.bfloat16

D_MODEL = 1024
DEPTH = 4
BRANCH_WIDTH = 512
N_BRANCH = 3
EPS = 1e-6
NEG_BIG = -1e30
HG_HEADS = 4
HG_DIM = 128
AT_HEADS = 4
AT_DIM = 128
IDX_HEADS = 8
IDX_DIM = 64
TOPK_MAX = 256
Q_BLOCK = 128
ROPE_THETA = 10000.0
S5_GROUP = 16
S5_GROUPS = BRANCH_WIDTH // S5_GROUP
S5_STATE = 64
N_IN = 8008

LANES = 128
HG_CHUNK = 128
HG_SUB = 8
S5_BLOCK = 16
KEY_GROUP = 256
VMEM_LIMIT = 52 * 1024 * 1024

REF_ALIGNED = 3840
REF_TAIL = 3912
COL_S5_U = 0
COL_S5_G = 512
COL_MERGE = 1024
COL_HG_Q = 4096
COL_HG_F = 4608
COL_HG_I = 5120
COL_HG_G = 5632
COL_AT_Q = 6144
COL_AT_K = 6656
COL_AT_V = 6784
COL_AT_G = 6912
COL_IX_Q = 7424
COL_IX_KW = 7936
N_PROJ = 8192
PROJ_TN = 2048
assert COL_HG_F % PROJ_TN + BRANCH_WIDTH <= PROJ_TN and COL_S5_U % PROJ_TN + BRANCH_WIDTH <= PROJ_TN

_NT = (((1,), (1,)), ((), ()))
_TN = (((0,), (0,)), ((), ()))


def _cparams(*sem):
    return pltpu.CompilerParams(dimension_semantics=sem, vmem_limit_bytes=VMEM_LIMIT)


def _sigmoid(x):
    return 1.0 / (1.0 + jnp.exp(-x))


def _silu(x):
    return x * _sigmoid(x)


def _lb_kernel(z_ref, o_ref):
    z = z_ref[...]
    e = jnp.exp(z - jnp.max(z, axis=0, keepdims=True))
    p = e / jnp.sum(e, axis=0, keepdims=True)
    acc = jnp.zeros_like(p[0:1])
    for l in range(z.shape[0]):
        acc = acc + p[l:l + 1]
        o_ref[l:l + 1, :] = acc - p[0:1]


def _lower_bounds(logits):
    return pl.pallas_call(
        _lb_kernel, out_shape=jax.ShapeDtypeStruct(logits.shape, F32), name="hg_lower_bounds",
    )(logits)


def _mod_kernel(c_ref, w_ref, b_ref, o_ref):
    c = c_ref[...]
    o_ref[0] = jnp.dot(_silu(c), w_ref[0], precision=lax.Precision.HIGHEST,
                       preferred_element_type=F32) + b_ref[0]


def _modulation(c, ada_w, ada_b):
    L, D, N = ada_w.shape
    B = c.shape[0]
    tn = 512
    return pl.pallas_call(
        _mod_kernel,
        grid=(L, N // tn),
        in_specs=[pl.BlockSpec((B, D), lambda l, n: (0, 0)),
                  pl.BlockSpec((1, D, tn), lambda l, n: (l, 0, n)),
                  pl.BlockSpec((1, 1, tn), lambda l, n: (l, 0, n))],
        out_specs=pl.BlockSpec((1, B, tn), lambda l, n: (l, 0, n)),
        out_shape=jax.ShapeDtypeStruct((L, B, N), F32),
        compiler_params=_cparams("parallel", "parallel"),
        name="adaln_modulation",
    )(c, ada_w, ada_b.reshape(L, 1, N))


def _wprep_kernel(w_ref, o_ref):
    tail = N_IN - REF_TAIL
    o_ref[0, :, 0:tail] = w_ref[0, :, REF_TAIL:N_IN].astype(BF16)
    o_ref[0, :, tail:tail + REF_ALIGNED] = w_ref[0, :, 0:REF_ALIGNED].astype(BF16)
    kw = w_ref[0, :, REF_ALIGNED:REF_ALIGNED + LANES]
    lane = lax.broadcasted_iota(jnp.int32, kw.shape, 1)
    o_ref[0, :, COL_IX_KW:COL_IX_KW + LANES] = jnp.where(lane < REF_TAIL - REF_ALIGNED, kw, 0.0).astype(BF16)
    o_ref[0, :, COL_IX_KW + LANES:N_PROJ] = jnp.zeros((kw.shape[0], N_PROJ - COL_IX_KW - LANES), BF16)


def _permute_w_in(w_in):
    L, D, N = w_in.shape
    tr = 128
    return pl.pallas_call(
        _wprep_kernel,
        grid=(L, D // tr),
        in_specs=[pl.BlockSpec((1, tr, N), lambda l, r: (l, r, 0))],
        out_specs=pl.BlockSpec((1, tr, N_PROJ), lambda l, r: (l, r, 0)),
        out_shape=jax.ShapeDtypeStruct((L, D, N_PROJ), BF16),
        compiler_params=_cparams("parallel", "parallel"),
        name="w_in_prepare",
    )(w_in)


def _proj_kernel(x_ref, g_ref, shift_ref, scale_ref, w_ref, o_ref, f_ref, u_ref, h_ref):
    j = pl.program_id(1)

    @pl.when(j == 0)
    def _():
        x = x_ref[...]
        r = lax.rsqrt(jnp.mean(x * x, axis=-1, keepdims=True) + EPS)
        h = (x * r) * g_ref[...] * (1.0 + scale_ref[0]) + shift_ref[0]
        h_ref[...] = h.astype(BF16)

    acc = jnp.dot(h_ref[...], w_ref[0], preferred_element_type=F32)
    o_ref[...] = acc.astype(BF16)

    @pl.when(j == COL_HG_F // PROJ_TN)
    def _():
        off = COL_HG_F % PROJ_TN
        f_ref[...] = acc[:, off:off + BRANCH_WIDTH]

    @pl.when(j == COL_S5_U // PROJ_TN)
    def _():
        off = COL_S5_U % PROJ_TN
        u_ref[...] = acc[:, off:off + BRANCH_WIDTH]


def _projection(x2, norm_g, shift, scale, w_all, layer, seq):
    M, D = x2.shape
    tm = min(1024, seq)
    per_batch = seq // tm
    return pl.pallas_call(
        _proj_kernel,
        grid=(M // tm, N_PROJ // PROJ_TN),
        in_specs=[pl.BlockSpec((tm, D), lambda i, j: (i, 0)),
                  pl.BlockSpec((1, D), lambda i, j: (0, 0)),
                  pl.BlockSpec((1, 1, D), lambda i, j: (i // per_batch, 0, 0)),
                  pl.BlockSpec((1, 1, D), lambda i, j: (i // per_batch, 0, 0)),
                  pl.BlockSpec((1, D, PROJ_TN), lambda i, j: (layer, 0, j))],
        out_specs=[pl.BlockSpec((tm, PROJ_TN), lambda i, j: (i, j)),
                   pl.BlockSpec((tm, BRANCH_WIDTH), lambda i, j: (i, 0)),
                   pl.BlockSpec((tm, BRANCH_WIDTH), lambda i, j: (i, 0))],
        out_shape=[jax.ShapeDtypeStruct((M, N_PROJ), BF16),
                   jax.ShapeDtypeStruct((M, BRANCH_WIDTH), F32),
                   jax.ShapeDtypeStruct((M, BRANCH_WIDTH), F32)],
        scratch_shapes=[pltpu.VMEM((tm, D), BF16)],
        compiler_params=_cparams("parallel", "arbitrary"),
        name="norm_in_proj",
    )(x2, norm_g, shift, scale, w_all)


def _split3(x):
    hi = x.astype(BF16)
    r1 = x - hi.astype(F32)
    mid = r1.astype(BF16)
    lo = (r1 - mid.astype(F32)).astype(BF16)
    return hi, mid, lo


def _hgrn_kernel(q_ref, f_ref, i_ref, g_ref, lb_ref, on_ref, o_ref, state_ref, *, rows):
    C, c = HG_CHUNK, HG_SUB
    nsub = C // c

    @pl.when(pl.program_id(2) == 0)
    def _():
        state_ref[...] = jnp.zeros_like(state_ref)

    lb = lb_ref[...]
    row_id = lax.broadcasted_iota(jnp.int32, (C, C), 0)
    col_id = lax.broadcasted_iota(jnp.int32, (C, C), 1)
    tril = jnp.where(col_id <= row_id, 1.0, 0.0).astype(BF16)
    sub_row = lax.broadcasted_iota(jnp.int32, (c, C), 0)
    sub_col = lax.broadcasted_iota(jnp.int32, (c, C), 1)

    for ci in range(rows // C):
        sl = slice(ci * C, (ci + 1) * C)
        q = q_ref[sl, :].astype(F32)
        v16 = i_ref[sl, :]
        fg = lb + (1.0 - lb) * _sigmoid(f_ref[sl, :])
        lf = jnp.log2(jnp.maximum(fg, 1e-30))
        k = 1.0 - fg
        hi, mid, lo = _split3(lf)
        b = (jnp.dot(tril, hi, preferred_element_type=F32)
             + jnp.dot(tril, mid, preferred_element_type=F32)
             + jnp.dot(tril, lo, preferred_element_type=F32))
        b_last = b[C - 1:C, :]

        state_t = state_ref[...]
        q0 = (q * jnp.exp2(b)).astype(BF16)
        o = lax.dot_general(q0, state_t.astype(BF16), _NT, preferred_element_type=F32)

        row_blocks = []
        k_run = []
        for t in range(nsub):
            r0 = t * c
            bt = b[r0:r0 + c, :]
            qt = q[r0:r0 + c, :]
            diag = jnp.zeros((c, C), F32)
            for s in range(c):
                e = jnp.exp2(bt - b[r0 + s:r0 + s + 1, :])
                col = jnp.sum(qt * e * k[r0 + s:r0 + s + 1, :], axis=-1, keepdims=True)
                diag = jnp.where(sub_col == r0 + s, col, diag)
            diag = jnp.where(sub_col - r0 <= sub_row, diag, 0.0)
            if t == 0:
                row_blocks.append(diag)
                continue
            r_t = b[r0 - 1:r0, :]
            if k_run:
                step = jnp.exp2(r_t - b[r0 - c - 1:r0 - c, :])
                k_run = [kb * step for kb in k_run]
            k_run.append(k[r0 - c:r0, :] * jnp.exp2(r_t - b[r0 - c:r0, :]))
            k_t = jnp.concatenate(k_run + [jnp.zeros((C - r0, C), F32)], axis=0).astype(BF16)
            q_t = (qt * jnp.exp2(bt - r_t)).astype(BF16)
            off = lax.dot_general(q_t, k_t, _NT, preferred_element_type=F32)
            row_blocks.append(off + diag)
        scores = jnp.concatenate(row_blocks, axis=0)
        o = o + jnp.dot(scores.astype(BF16), v16, preferred_element_type=F32)

        k_dec = (k * jnp.exp2(b_last - b)).astype(BF16)
        state_ref[...] = (state_t * jnp.exp2(b_last)
                          + lax.dot_general(v16, k_dec, _TN, preferred_element_type=F32))

        r = lax.rsqrt(jnp.mean(o * o, axis=-1, keepdims=True) + EPS)
        o_ref[sl, :] = ((o * r) * on_ref[...] * _silu(g_ref[sl, :].astype(F32))).astype(BF16)


def _hgrn(proj, hgf, lb, onorm_g, batch, seq):
    rows = min(512, seq)
    nr = seq // rows
    cb = lambda base: (lambda b, h, r: (b * nr + r, base // HG_DIM + h))
    return pl.pallas_call(
        functools.partial(_hgrn_kernel, rows=rows),
        grid=(batch, HG_HEADS, nr),
        in_specs=[pl.BlockSpec((rows, HG_DIM), cb(COL_HG_Q)),
                  pl.BlockSpec((rows, HG_DIM), cb(0)),
                  pl.BlockSpec((rows, HG_DIM), cb(COL_HG_I)),
                  pl.BlockSpec((rows, HG_DIM), cb(COL_HG_G)),
                  pl.BlockSpec((1, HG_DIM), lambda b, h, r: (0, h)),
                  pl.BlockSpec((1, HG_DIM), lambda b, h, r: (0, 0))],
        out_specs=pl.BlockSpec((rows, HG_DIM), lambda b, h, r: (b * nr + r, h)),
        out_shape=jax.ShapeDtypeStruct((batch * seq, BRANCH_WIDTH), BF16),
        scratch_shapes=[pltpu.VMEM((HG_DIM, HG_DIM), F32)],
        compiler_params=_cparams("parallel", "parallel", "arbitrary"),
        name="hgrn2_mixer",
    )(proj, hgf, proj, proj, lb, onorm_g)


def _rope_tables(positions):
    pos = positions.astype(F32)[..., None]

    def tables(dim, reps):
        inv = ROPE_THETA ** (-jnp.arange(0, dim, 2, dtype=F32) / dim)
        ang = pos * inv
        c, s = jnp.cos(ang), jnp.sin(ang)
        return (jnp.concatenate([c, c] * reps, axis=-1),
                jnp.concatenate([-s, s] * reps, axis=-1))

    return tables(AT_DIM, 1) + tables(IDX_DIM, LANES // IDX_DIM)


def _rope_full(x, cos, sin_signed):
    return x * cos + pltpu.roll(x, AT_DIM // 2, 1) * sin_signed


def _rope_idx(x, cos, sin_signed, first_half):
    h = IDX_DIM // 2
    partner = jnp.where(first_half, pltpu.roll(x, LANES - h, 1), pltpu.roll(x, h, 1))
    return x * cos + partner * sin_signed


def _col_reduce(x, op):
    part = 64
    if x.shape[0] > part:
        x = op(x.reshape(x.shape[0] // part, part, x.shape[1]), axis=0)
    return op(x, axis=0, keepdims=True)


def _dsa_block(width, j, g_refs, w_rows, o_ref, k_s, vt_s, ik_s, xm_s, q4_s, key_s, val_s, lg_s, p_s,
               *, topk):
    QB = Q_BLOCK
    nch = width // LANES
    chunk = lambda c: slice(c * LANES, (c + 1) * LANES)
    key_w = key_s.at[0:width, :]
    qpos = j * QB + lax.broadcasted_iota(jnp.int32, (1, QB), 1)
    krow = lax.broadcasted_iota(jnp.int32, (LANES, 1), 0)

    for c in range(nch):
        rel = lax.dot_general(ik_s[chunk(c), :], xm_s[...], _NT, preferred_element_type=F32)
        acc = jnp.zeros((LANES, QB), F32)
        for h in range(IDX_HEADS):
            acc = acc + jnp.maximum(rel[:, h * QB:(h + 1) * QB], 0.0) * w_rows[IDX_DIM + h:IDX_DIM + h + 1, :]
        key_s[chunk(c), :] = jnp.where(c * LANES + krow <= qpos, acc, NEG_BIG)

    kf = float(topk)
    if width <= topk:
        for c in range(nch):
            val_s[chunk(c), :] = jnp.where(c * LANES + krow <= qpos, 1.0, 0.0)
    else:
        def as_score(image):
            bits = jnp.where(image < 0, image ^ jnp.int32(0x7FFFFFFF), image)
            return pltpu.bitcast(bits, F32)

        def count_ge(cand):
            return _col_reduce(jnp.where(key_w[...] >= as_score(cand), 1.0, 0.0), jnp.sum)

        zero = jnp.zeros((1, QB), jnp.int32)
        ans0 = jnp.where(count_ge(zero) >= kf, zero, jnp.full((1, QB), -2**31, jnp.int32))

        def bisect(i, ans):
            cand = ans + (jnp.int32(1) << (30 - i))
            return jnp.where(count_ge(cand) >= kf, cand, ans)

        thr = as_score(lax.fori_loop(0, 31, bisect, ans0))

        above = jnp.zeros((LANES, QB), F32)
        for c in range(nch):
            above = above + jnp.where(key_s[chunk(c), :] > thr, 1.0, 0.0)
        need = kf - jnp.sum(above, axis=0, keepdims=True)
        rr = lax.broadcasted_iota(jnp.int32, (LANES, LANES), 0)
        cc = lax.broadcasted_iota(jnp.int32, (LANES, LANES), 1)
        lower = jnp.where(cc <= rr, 1.0, 0.0).astype(BF16)
        seen = jnp.zeros((1, QB), F32)
        for c in range(nch):
            key = key_s[chunk(c), :]
            eq_c = jnp.where(key == thr, 1.0, 0.0)
            rank = jnp.dot(lower, eq_c.astype(BF16), preferred_element_type=F32) + seen
            take = jnp.where(key > thr, 1.0, jnp.where(rank <= need, eq_c, 0.0))
            val_s[chunk(c), :] = jnp.where(c * LANES + krow <= qpos, take, 0.0)
            seen = rank[LANES - 1:LANES, :]

    for h in range(AT_HEADS):
        hs = slice(h * AT_DIM, (h + 1) * AT_DIM)
        qh = q4_s[hs, :]
        top = jnp.full((LANES, QB), NEG_BIG, F32)
        for c in range(nch):
            lg = lax.dot_general(k_s[chunk(c), :], qh, _NT, preferred_element_type=F32)
            lg = jnp.where(val_s[chunk(c), :] > 0.0, lg, NEG_BIG)
            lg_s[chunk(c), :] = lg
            top = jnp.maximum(top, lg)
        top = jnp.max(top, axis=0, keepdims=True)
        tot = jnp.zeros((LANES, QB), F32)
        for c in range(nch):
            p = jnp.exp2(lg_s[chunk(c), :] - top)
            tot = tot + p
            p_s[chunk(c), :] = p.astype(BF16)
        denom = jnp.sum(tot, axis=0, keepdims=True)
        oh_t = jnp.dot(vt_s[:, 0:width], p_s[0:width, :], preferred_element_type=F32) / denom
        g_ref = g_refs[h // 2]
        goff = (h % 2) * AT_DIM
        o_ref[:, hs] = (oh_t.T * _silu(g_ref[:, goff:goff + AT_DIM].astype(F32))).astype(BF16)


def _dsa_kernel(q_ref, g0_ref, g1_ref, iq0_ref, iq1_ref, k_ref, v_ref, ikw_ref,
                ca_ref, sa_ref, ci_ref, si_ref, qn_ref, kn_ref, o_ref,
                k_s, vt_s, ik_s, xm_s, q4_s, key_s, val_s, lg_s, p_s, *, seq, topk):
    j = pl.program_id(1)
    QB = Q_BLOCK
    lane = lax.broadcasted_iota(jnp.int32, (1, LANES), 1)
    first_half = (lane % IDX_DIM) < (IDX_DIM // 2)
    low_group = lane < IDX_DIM

    @pl.when(j == 0)
    def _():
        kk = k_ref[...].astype(F32)
        r = lax.rsqrt(jnp.mean(kk * kk, axis=-1, keepdims=True) + EPS)
        kk = (kk * r) * kn_ref[...]
        k_s[...] = _rope_full(kk, ca_ref[0], sa_ref[0]).astype(BF16)
        vt_s[...] = v_ref[...].astype(F32).T.astype(BF16)
        ik = _rope_idx(ikw_ref[...].astype(F32), ci_ref[0], si_ref[0], first_half)
        ik_s[...] = jnp.where(low_group, ik, pltpu.roll(ik, IDX_DIM, 1)).astype(BF16)

    rows = pl.ds(pl.multiple_of(j * QB, QB), QB)
    ca, sa = ca_ref[0, rows, :], sa_ref[0, rows, :]
    ci, si = ci_ref[0, rows, :], si_ref[0, rows, :]
    w_rows = ikw_ref[rows, :].astype(F32).T

    per_slab = LANES // IDX_DIM
    for m in range(IDX_HEADS // per_slab):
        iq_ref = (iq0_ref, iq1_ref)[m // 2]
        off = (m % 2) * LANES
        xr = _rope_idx(iq_ref[:, off:off + LANES].astype(F32), ci, si, first_half)
        for par in range(per_slab):
            h = m * per_slab + par
            xm = jnp.where(low_group if par == 0 else jnp.logical_not(low_group), xr, 0.0)
            xm_s[h * QB:(h + 1) * QB, :] = xm.astype(BF16)
    scale = math.log2(math.e) / math.sqrt(AT_DIM)
    for h in range(AT_HEADS):
        hs = slice(h * AT_DIM, (h + 1) * AT_DIM)
        qh = q_ref[:, hs].astype(F32)
        r = lax.rsqrt(jnp.mean(qh * qh, axis=-1, keepdims=True) + EPS)
        q4_s[hs, :] = (_rope_full((qh * r) * qn_ref[...], ca, sa) * scale).astype(BF16)

    group = min(KEY_GROUP, seq)
    per_group = group // QB
    for n in range(1, seq // group + 1):
        @pl.when(j // per_group + 1 == n)
        def _(n=n):
            _dsa_block(n * group, j, (g0_ref, g1_ref), w_rows, o_ref, k_s, vt_s, ik_s, xm_s, q4_s,
                       key_s, val_s, lg_s, p_s, topk=topk)


def _dsa(proj, tables, qn_g, kn_g, batch, seq):
    nb = seq // Q_BLOCK
    topk = min(TOPK_MAX, seq // 4)
    ca, sa, ci, si = tables
    half = BRANCH_WIDTH // 2
    qrow = lambda width, col: pl.BlockSpec((Q_BLOCK, width), lambda b, j: (b * nb + j, col // width))
    krow = lambda col: pl.BlockSpec((seq, LANES), lambda b, j: (b, col // LANES))
    tab = pl.BlockSpec((1, seq, LANES), lambda b, j: (b, 0, 0))
    vec = pl.BlockSpec((1, LANES), lambda b, j: (0, 0))
    return pl.pallas_call(
        functools.partial(_dsa_kernel, seq=seq, topk=topk),
        grid=(batch, nb),
        in_specs=[qrow(BRANCH_WIDTH, COL_AT_Q),
                  qrow(half, COL_AT_G), qrow(half, COL_AT_G + half),
                  qrow(half, COL_IX_Q), qrow(half, COL_IX_Q + half),
                  krow(COL_AT_K), krow(COL_AT_V), krow(COL_IX_KW),
                  tab, tab, tab, tab, vec, vec],
        out_specs=pl.BlockSpec((Q_BLOCK, BRANCH_WIDTH), lambda b, j: (b * nb + j, 0)),
        out_shape=jax.ShapeDtypeStruct((batch * seq, BRANCH_WIDTH), BF16),
        scratch_shapes=[pltpu.VMEM((seq, LANES), BF16), pltpu.VMEM((LANES, seq), BF16),
                        pltpu.VMEM((seq, LANES), BF16),
                        pltpu.VMEM((IDX_HEADS * Q_BLOCK, LANES), BF16),
                        pltpu.VMEM((AT_HEADS * Q_BLOCK, LANES), BF16),
                        pltpu.VMEM((seq, Q_BLOCK), F32), pltpu.VMEM((seq, Q_BLOCK), F32),
                        pltpu.VMEM((seq, Q_BLOCK), F32), pltpu.VMEM((seq, Q_BLOCK), BF16)],
        compiler_params=_cparams("parallel", "arbitrary"),
        name="dsa_mixer",
    )(proj, proj, proj, proj, proj, proj, proj, proj, ca, sa, ci, si, qn_g, kn_g)


def _s5prep_kernel(are_ref, aim_ref, ldt_ref, btr_ref, bti_ref, cr_ref, ci_ref, perm_ref,
                   t_ref, min_ref, nout_ref, al_ref):
    Lb, G16 = S5_BLOCK, S5_GROUP
    hp = lax.Precision.HIGHEST
    a_re, a_im = are_ref[0], aim_ref[0]
    dt = jnp.exp(ldt_ref[0])
    mag = jnp.exp(a_re * dt)
    ang = a_im * dt
    ab_r, ab_i = mag * jnp.cos(ang), mag * jnp.sin(ang)
    nr, ni = ab_r - 1.0, ab_i
    den = a_re * a_re + a_im * a_im
    fr = (nr * a_re + ni * a_im) / den
    fi = (ni * a_re - nr * a_im) / den
    bt_r, bt_i = btr_ref[0], bti_ref[0]
    bb_r = fr * bt_r - fi * bt_i
    bb_i = fr * bt_i + fi * bt_r
    c_r, c_i = cr_ref[0], ci_ref[0]

    pw_r, pw_i = [jnp.ones_like(ab_r)], [jnp.zeros_like(ab_r)]
    for _ in range(Lb):
        pr, pi = pw_r[-1], pw_i[-1]
        pw_r.append(pr * ab_r - pi * ab_i)
        pw_i.append(pr * ab_i + pi * ab_r)

    def readout(first):
        re = jnp.concatenate([c_r * pw_r[first + t] - c_i * pw_i[first + t] for t in range(Lb)], axis=0)
        im = jnp.concatenate([c_r * pw_i[first + t] + c_i * pw_r[first + t] for t in range(Lb)], axis=0)
        return re, im

    wc_r, wc_i = readout(0)
    taps = (lax.dot_general(bb_r, wc_r, _NT, precision=hp, preferred_element_type=F32)
            - lax.dot_general(bb_i, wc_i, _NT, precision=hp, preferred_element_type=F32))
    t_mat = jnp.concatenate(
        [taps] + [jnp.concatenate([jnp.zeros((G16, s * G16), F32), taps[:, :(Lb - s) * G16]], axis=1)
                  for s in range(1, Lb)], axis=0)
    m_r = jnp.concatenate([pw_r[Lb - 1 - s] * bb_r - pw_i[Lb - 1 - s] * bb_i for s in range(Lb)], axis=0)
    m_i = jnp.concatenate([pw_r[Lb - 1 - s] * bb_i + pw_i[Lb - 1 - s] * bb_r for s in range(Lb)], axis=0)
    m_in = jnp.concatenate([m_r, m_i, m_i, m_r], axis=1)
    n_r, n_i = readout(1)
    n_out = jnp.concatenate([n_r, -n_i], axis=1)

    perm = perm_ref[0]
    t_rows = jnp.dot(perm, t_mat.astype(BF16), preferred_element_type=F32).astype(BF16)
    t_ref[0] = lax.dot_general(t_rows, perm, _NT, preferred_element_type=F32).astype(BF16)
    min_ref[0] = jnp.dot(perm, m_in.astype(BF16), preferred_element_type=F32).astype(BF16)
    nout_ref[0] = jnp.dot(perm, n_out.astype(BF16), preferred_element_type=F32).astype(BF16)
    al_r, al_i = pw_r[Lb], pw_i[Lb]
    al_ref[0] = jnp.concatenate([jnp.concatenate([al_r, al_r], axis=1),
                                 jnp.concatenate([-al_i, al_i], axis=1),
                                 jnp.zeros((6, 2 * S5_STATE), F32)], axis=0)


def _s5_prepare(a_re, a_im, log_dt, b_re, b_im, c_re, c_im):
    G, P, K = S5_GROUPS, S5_STATE, S5_BLOCK * S5_GROUP
    half = LANES // S5_GROUP
    g3 = lambda shape: pl.BlockSpec((1,) + shape, lambda g: (g, 0, 0))
    return pl.pallas_call(
        _s5prep_kernel,
        grid=(G,),
        in_specs=[g3((1, P)), g3((1, P)), g3((1, 1)), g3((S5_GROUP, P)), g3((S5_GROUP, P)),
                  g3((S5_GROUP, P)), g3((S5_GROUP, P)),
                  pl.BlockSpec((1, K, K), lambda g: (g % half, 0, 0))],
        out_specs=[g3((K, K)), g3((K, 4 * P)), g3((K, 2 * P)), g3((8, 2 * P))],
        out_shape=[jax.ShapeDtypeStruct((G, K, K), BF16), jax.ShapeDtypeStruct((G, K, 4 * P), BF16),
                   jax.ShapeDtypeStruct((G, K, 2 * P), BF16), jax.ShapeDtypeStruct((G, 8, 2 * P), F32)],
        compiler_params=_cparams("parallel"),
        name="s5_prepare",
    )(a_re.reshape(G, 1, P), a_im.reshape(G, 1, P), log_dt.reshape(G, 1, 1),
      jnp.swapaxes(b_re, 1, 2), jnp.swapaxes(b_im, 1, 2), c_re, c_im, jnp.asarray(_s5_slot_perm(), BF16))


def _s5_slot(step, g):
    half = LANES // S5_GROUP
    return (step // half) * half + (g % half + step % half) % half


def _s5_slot_perm():
    half = LANES // S5_GROUP
    perm = np.zeros((half, S5_BLOCK * S5_GROUP, S5_BLOCK * S5_GROUP), np.float32)
    for g in range(half):
        for step in range(S5_BLOCK):
            for ch in range(S5_GROUP):
                perm[g, _s5_slot(step, g) * S5_GROUP + ch, step * S5_GROUP + ch] = 1.0
    return perm


def _s5_in_kernel(*refs, nblk):
    u_refs, o_ref = refs[:-1], refs[-1]
    half = LANES // S5_GROUP
    for k, u_ref in enumerate(u_refs):
        for step in range(S5_BLOCK):
            slab = u_ref[pl.ds(step, nblk, stride=S5_BLOCK), :]
            shift = (step % half) * S5_GROUP
            if shift:
                slab = pltpu.roll(slab, shift, 1)
            slab = slab.astype(BF16)
            for gi in range(half):
                g = k * half + gi
                lo = (_s5_slot(step, g) % half) * S5_GROUP
                base = (step // half) * LANES
                o_ref[g, :, base + lo:base + lo + S5_GROUP] = slab[:, lo:lo + S5_GROUP]


def _s5_out_kernel(*refs, nblk):
    y_ref, o_refs = refs[0], refs[1:]
    half = LANES // S5_GROUP
    slot_of_lane = lax.broadcasted_iota(jnp.int32, (1, LANES), 1) // S5_GROUP
    for k, o_ref in enumerate(o_refs):
        for step in range(S5_BLOCK):
            base = (step // half) * LANES
            slab = None
            for gi in range(half):
                g = k * half + gi
                src = y_ref[g, :, base:base + LANES]
                slab = src if slab is None else jnp.where(slot_of_lane == _s5_slot(step, g) % half, src, slab)
            slab = slab.astype(F32)
            shift = (step % half) * S5_GROUP
            if shift:
                slab = pltpu.roll(slab, LANES - shift, 1)
            o_ref[pl.ds(step, nblk, stride=S5_BLOCK), :] = slab


def _s5_kernel(u_ref, t_ref, min_ref, nout_ref, al_ref, y_ref, inj_s, injx_s, xp_s, *, batch, nblk):
    P2 = 2 * S5_STATE
    u = u_ref[0]
    inj = jnp.dot(u, min_ref[0], preferred_element_type=F32)
    inj_s[...] = inj[:, 0:P2]
    injx_s[...] = inj[:, P2:2 * P2]
    a_same = al_ref[0, 0:1, :]
    a_cross = al_ref[0, 1:2, :]

    def step(jb, carry):
        x, xs = carry
        rows = pl.ds(jb, batch, stride=nblk)
        xp_s[rows, :] = x
        return (x * a_same + xs * a_cross + inj_s[rows, :],
                xs * a_same - x * a_cross + injx_s[rows, :])

    z = jnp.zeros((batch, P2), F32)
    lax.fori_loop(0, nblk, step, (z, z))
    y = jnp.dot(u, t_ref[0], preferred_element_type=F32)
    y = y + lax.dot_general(xp_s[...].astype(BF16), nout_ref[0], _NT, preferred_element_type=F32)
    y_ref[0] = y.astype(BF16)


def _s5_scan(u32, ops, batch, seq):
    t_mat, m_in, n_out, al = ops
    G, P, K = S5_GROUPS, S5_STATE, S5_BLOCK * S5_GROUP
    nblk = seq // S5_BLOCK
    R = nblk * batch
    nslab = BRANCH_WIDTH // LANES
    slab = lambda k: pl.BlockSpec((seq, LANES), lambda b, k=k: (b, k))
    grouped = pl.BlockSpec((G, nblk, K), lambda b: (0, b, 0))
    u = pl.pallas_call(
        functools.partial(_s5_in_kernel, nblk=nblk),
        grid=(batch,),
        in_specs=[slab(k) for k in range(nslab)],
        out_specs=grouped,
        out_shape=jax.ShapeDtypeStruct((G, R, K), BF16),
        compiler_params=_cparams("parallel"),
        name="s5_relayout_in",
    )(*([u32] * nslab))
    g3 = lambda shape: pl.BlockSpec((1,) + shape, lambda g: (g, 0, 0))
    y = pl.pallas_call(
        functools.partial(_s5_kernel, batch=batch, nblk=nblk),
        grid=(G,),
        in_specs=[g3((R, K)), g3((K, K)), g3((K, 4 * P)), g3((K, 2 * P)), g3((8, 2 * P))],
        out_specs=g3((R, K)),
        out_shape=jax.ShapeDtypeStruct((G, R, K), BF16),
        scratch_shapes=[pltpu.VMEM((R, 2 * P), F32), pltpu.VMEM((R, 2 * P), F32),
                        pltpu.VMEM((R, 2 * P), F32)],
        compiler_params=_cparams("parallel"),
        name="s5_scan",
    )(u, t_mat, m_in, n_out, al)
    return pl.pallas_call(
        functools.partial(_s5_out_kernel, nblk=nblk),
        grid=(batch,),
        in_specs=[grouped],
        out_specs=[pl.BlockSpec((seq, LANES), lambda b: (b, 0)) for _ in range(nslab)],
        out_shape=[jax.ShapeDtypeStruct((batch * seq, LANES), F32) for _ in range(nslab)],
        compiler_params=_cparams("parallel"),
        name="s5_relayout_out",
    )(y)


def _merge_kernel(x_ref, gate_ref, ya_ref, yb_ref, y50_ref, y51_ref, y52_ref, y53_ref, u_ref, sg_ref,
                  m0_ref, m1_ref, m2_ref, dsk_ref, gw_ref, gb_ref, wb_ref, wo_ref, o_ref):
    y5 = jnp.concatenate([y50_ref[...], y51_ref[...], y52_ref[...], y53_ref[...]], axis=1)
    y = y5 + dsk_ref[...] * u_ref[...].astype(F32)
    y = jax.nn.gelu(y)
    glu = jnp.dot(y.astype(BF16), gw_ref[...], preferred_element_type=F32) + gb_ref[...]
    yc = (y * _sigmoid(glu) * _silu(sg_ref[...].astype(F32))).astype(BF16)
    merged = jnp.zeros(x_ref.shape, F32)
    for n, (yn, m_ref) in enumerate(((ya_ref[...], m0_ref), (yb_ref[...], m1_ref), (yc, m2_ref))):
        yd = jnp.dot(yn, wb_ref[n], preferred_element_type=F32)
        merged = merged + _sigmoid(m_ref[...].astype(F32)) * yd
    o_ref[...] = x_ref[...] + gate_ref[0] * jnp.dot(merged.astype(BF16), wo_ref[...],
                                                    preferred_element_type=F32)


def _merge(x2, gate, ya, yb, y5, proj, d_skip, glu_w, glu_b, w_branch, w_out, seq):
    M, D = x2.shape
    W = BRANCH_WIDTH
    tm = min(512, seq)
    per_batch = seq // tm
    row = lambda cols, colblk: pl.BlockSpec((tm, cols), lambda i: (i, colblk))
    full = lambda shape: pl.BlockSpec(shape, lambda i: (0,) * len(shape))
    return pl.pallas_call(
        _merge_kernel,
        grid=(M // tm,),
        in_specs=[row(D, 0),
                  pl.BlockSpec((1, 1, D), lambda i: (i // per_batch, 0, 0)),
                  row(W, 0), row(W, 0),
                  row(LANES, 0), row(LANES, 0), row(LANES, 0), row(LANES, 0),
                  row(W, COL_S5_U // W), row(W, COL_S5_G // W),
                  row(D, COL_MERGE // D), row(D, COL_MERGE // D + 1), row(D, COL_MERGE // D + 2),
                  full((1, W)), full((W, W)), full((1, W)), full((N_BRANCH, W, D)), full((D, D))],
        out_specs=row(D, 0),
        out_shape=jax.ShapeDtypeStruct((M, D), F32),
        compiler_params=_cparams("parallel"),
        name="merge_out",
    )(x2, gate, ya, yb, *y5, proj, proj, proj, proj, proj, d_skip, glu_w, glu_b, w_branch, w_out)


def kernel(x, c, positions, ada_w, ada_b, norm_g, w_in, hg_lb_logits, hg_onorm_g, at_qnorm_g,
           at_knorm_g, s5_a_re, s5_a_im, s5_log_dt, s5_b_re, s5_b_im, s5_c_re, s5_c_im, s5_d,
           s5_glu_w, s5_glu_b, w_branch, w_out):
    B, S, D = x.shape
    L = ada_w.shape[0]
    lb_all = _lower_bounds(hg_lb_logits.astype(F32))
    mod = _modulation(c, ada_w, ada_b)
    w_all = _permute_w_in(w_in)
    tables = _rope_tables(positions)
    x2 = x.reshape(B * S, D)
    for l in range(L):
        shift = mod[l, :, None, 0:D]
        scale = mod[l, :, None, D:2 * D]
        gate = mod[l, :, None, 2 * D:3 * D]
        proj, hgf, u32 = _projection(x2, norm_g[l][None, :], shift, scale, w_all, l, S)
        ya = _hgrn(proj, hgf, lb_all[l][None, :], hg_onorm_g[l][None, :], B, S)
        yb = _dsa(proj, tables, at_qnorm_g[l][None, :], at_knorm_g[l][None, :], B, S)
        ops = _s5_prepare(s5_a_re[l], s5_a_im[l], s5_log_dt[l], s5_b_re[l], s5_b_im[l],
                          s5_c_re[l], s5_c_im[l])
        y5 = _s5_scan(u32, ops, B, S)
        x2 = _merge(x2, gate, ya, yb, y5, proj, s5_d[l][None, :], s5_glu_w[l].astype(BF16),
                    s5_glu_b[l][None, :], w_branch[l].astype(BF16), w_out[l].astype(BF16), S)
    return x2.reshape(B, S, D)
```

```python
import functools
import math

import jax
import jax.numpy as jnp
import numpy as np
from jax import lax
from jax.experimental import pallas as pl
from jax.experimental.pallas import tpu as pltpu

F32 = jnp.float32
BF16 = jnp.bfloat16

D_MODEL = 1024
DEPTH = 4
BRANCH_WIDTH = 512
N_BRANCH = 3
EPS = 1e-6
NEG_BIG = -1e30
HG_HEADS = 4
HG_DIM = 128
AT_HEADS = 4
AT_DIM = 128
IDX_HEADS = 8
IDX_DIM = 64
TOPK_MAX = 256
Q_BLOCK = 128
ROPE_THETA = 10000.0
S5_GROUP = 16
S5_GROUPS = BRANCH_WIDTH // S5_GROUP
S5_STATE = 64
N_IN = 8008

LANES = 128
HG_CHUNK = 128
HG_SUB = 8
S5_BLOCK = 16
KEY_GROUP = 256
VMEM_LIMIT = 52 * 1024 * 1024

REF_ALIGNED = 3840
REF_TAIL = 3912
COL_S5_U = 0
COL_S5_G = 512
COL_MERGE = 1024
COL_HG_Q = 4096
COL_HG_F = 4608
COL_HG_I = 5120
COL_HG_G = 5632
COL_AT_Q = 6144
COL_AT_K = 6656
COL_AT_V = 6784
COL_AT_G = 6912
COL_IX_Q = 7424
COL_IX_KW = 7936
N_PROJ = 8192
PROJ_TN = 2048
assert COL_HG_F % PROJ_TN + BRANCH_WIDTH <= PROJ_TN and COL_S5_U % PROJ_TN + BRANCH_WIDTH <= PROJ_TN

_NT = (((1,), (1,)), ((), ()))
_TN = (((0,), (0,)), ((), ()))


def _cparams(*sem):
    return pltpu.CompilerParams(dimension_semantics=sem, vmem_limit_bytes=VMEM_LIMIT)


def _sigmoid(x):
    return 1.0 / (1.0 + jnp.exp(-x))


def _silu(x):
    return x * _sigmoid(x)


def _lb_kernel(z_ref, o_ref):
    z = z_ref[...]
    e = jnp.exp(z - jnp.max(z, axis=0, keepdims=True))
    p = e / jnp.sum(e, axis=0, keepdims=True)
    acc = jnp.zeros_like(p[0:1])
    for l in range(z.shape[0]):
        acc = acc + p[l:l + 1]
        o_ref[l:l + 1, :] = acc - p[0:1]


def _lower_bounds(logits):
    return pl.pallas_call(
        _lb_kernel, out_shape=jax.ShapeDtypeStruct(logits.shape, F32), name="hg_lower_bounds",
    )(logits)


def _mod_kernel(c_ref, w_ref, b_ref, o_ref):
    c = c_ref[...]
    o_ref[0] = jnp.dot(_silu(c), w_ref[0], precision=lax.Precision.HIGHEST,
                       preferred_element_type=F32) + b_ref[0]


def _modulation(c, ada_w, ada_b):
    L, D, N = ada_w.shape
    B = c.shape[0]
    tn = 512
    return pl.pallas_call(
        _mod_kernel,
        grid=(L, N // tn),
        in_specs=[pl.BlockSpec((B, D), lambda l, n: (0, 0)),
                  pl.BlockSpec((1, D, tn), lambda l, n: (l, 0, n)),
                  pl.BlockSpec((1, 1, tn), lambda l, n: (l, 0, n))],
        out_specs=pl.BlockSpec((1, B, tn), lambda l, n: (l, 0, n)),
        out_shape=jax.ShapeDtypeStruct((L, B, N), F32),
        compiler_params=_cparams("parallel", "parallel"),
        name="adaln_modulation",
    )(c, ada_w, ada_b.reshape(L, 1, N))


def _wprep_kernel(w_ref, o_ref):
    tail = N_IN - REF_TAIL
    o_ref[0, :, 0:tail] = w_ref[0, :, REF_TAIL:N_IN].astype(BF16)
    o_ref[0, :, tail:tail + REF_ALIGNED] = w_ref[0, :, 0:REF_ALIGNED].astype(BF16)
    kw = w_ref[0, :, REF_ALIGNED:REF_ALIGNED + LANES]
    lane = lax.broadcasted_iota(jnp.int32, kw.shape, 1)
    o_ref[0, :, COL_IX_KW:COL_IX_KW + LANES] = jnp.where(lane < REF_TAIL - REF_ALIGNED, kw, 0.0).astype(BF16)
    o_ref[0, :, COL_IX_KW + LANES:N_PROJ] = jnp.zeros((kw.shape[0], N_PROJ - COL_IX_KW - LANES), BF16)


def _permute_w_in(w_in):
    L, D, N = w_in.shape
    tr = 128
    return pl.pallas_call(
        _wprep_kernel,
        grid=(L, D // tr),
        in_specs=[pl.BlockSpec((1, tr, N), lambda l, r: (l, r, 0))],
        out_specs=pl.BlockSpec((1, tr, N_PROJ), lambda l, r: (l, r, 0)),
        out_shape=jax.ShapeDtypeStruct((L, D, N_PROJ), BF16),
        compiler_params=_cparams("parallel", "parallel"),
        name="w_in_prepare",
    )(w_in)


def _proj_kernel(x_ref, g_ref, shift_ref, scale_ref, w_ref, o_ref, f_ref, u_ref, h_ref):
    j = pl.program_id(1)

    @pl.when(j == 0)
    def _():
        x = x_ref[...]
        r = lax.rsqrt(jnp.mean(x * x, axis=-1, keepdims=True) + EPS)
        h = (x * r) * g_ref[...] * (1.0 + scale_ref[0]) + shift_ref[0]
        h_ref[...] = h.astype(BF16)

    acc = jnp.dot(h_ref[...], w_ref[0], preferred_element_type=F32)
    o_ref[...] = acc.astype(BF16)

    @pl.when(j == COL_HG_F // PROJ_TN)
    def _():
        off = COL_HG_F % PROJ_TN
        f_ref[...] = acc[:, off:off + BRANCH_WIDTH]

    @pl.when(j == COL_S5_U // PROJ_TN)
    def _():
        off = COL_S5_U % PROJ_TN
        u_ref[...] = acc[:, off:off + BRANCH_WIDTH]


def _projection(x2, norm_g, shift, scale, w_all, layer, seq):
    M, D = x2.shape
    tm = min(1024, seq)
    per_batch = seq // tm
    return pl.pallas_call(
        _proj_kernel,
        grid=(M // tm, N_PROJ // PROJ_TN),
        in_specs=[pl.BlockSpec((tm, D), lambda i, j: (i, 0)),
                  pl.BlockSpec((1, D), lambda i, j: (0, 0)),
                  pl.BlockSpec((1, 1, D), lambda i, j: (i // per_batch, 0, 0)),
                  pl.BlockSpec((1, 1, D), lambda i, j: (i // per_batch, 0, 0)),
                  pl.BlockSpec((1, D, PROJ_TN), lambda i, j: (layer, 0, j))],
        out_specs=[pl.BlockSpec((tm, PROJ_TN), lambda i, j: (i, j)),
                   pl.BlockSpec((tm, BRANCH_WIDTH), lambda i, j: (i, 0)),
                   pl.BlockSpec((tm, BRANCH_WIDTH), lambda i, j: (i, 0))],
        out_shape=[jax.ShapeDtypeStruct((M, N_PROJ), BF16),
                   jax.ShapeDtypeStruct((M, BRANCH_WIDTH), F32),
                   jax.ShapeDtypeStruct((M, BRANCH_WIDTH), F32)],
        scratch_shapes=[pltpu.VMEM((tm, D), BF16)],
        compiler_params=_cparams("parallel", "arbitrary"),
        name="norm_in_proj",
    )(x2, norm_g, shift, scale, w_all)


def _split3(x):
    hi = x.astype(BF16)
    r1 = x - hi.astype(F32)
    mid = r1.astype(BF16)
    lo = (r1 - mid.astype(F32)).astype(BF16)
    return hi, mid, lo


def _hgrn_kernel(q_ref, f_ref, i_ref, g_ref, lb_ref, on_ref, o_ref, state_ref, *, rows):
    C, c = HG_CHUNK, HG_SUB
    nsub = C // c

    @pl.when(pl.program_id(2) == 0)
    def _():
        state_ref[...] = jnp.zeros_like(state_ref)

    lb = lb_ref[...]
    row_id = lax.broadcasted_iota(jnp.int32, (C, C), 0)
    col_id = lax.broadcasted_iota(jnp.int32, (C, C), 1)
    tril = jnp.where(col_id <= row_id, 1.0, 0.0).astype(BF16)
    sub_row = lax.broadcasted_iota(jnp.int32, (c, C), 0)
    sub_col = lax.broadcasted_iota(jnp.int32, (c, C), 1)

    for ci in range(rows // C):
        sl = slice(ci * C, (ci + 1) * C)
        q = q_ref[sl, :].astype(F32)
        v16 = i_ref[sl, :]
        fg = lb + (1.0 - lb) * _sigmoid(f_ref[sl, :])
        lf = jnp.log2(jnp.maximum(fg, 1e-30))
        k = 1.0 - fg
        hi, mid, lo = _split3(lf)
        b = (jnp.dot(tril, hi, preferred_element_type=F32)
             + jnp.dot(tril, mid, preferred_element_type=F32)
             + jnp.dot(tril, lo, preferred_element_type=F32))
        b_last = b[C - 1:C, :]

        state_t = state_ref[...]
        q0 = (q * jnp.exp2(b)).astype(BF16)
        o = lax.dot_general(q0, state_t.astype(BF16), _NT, preferred_element_type=F32)

        row_blocks = []
        k_run = []
        for t in range(nsub):
            r0 = t * c
            bt = b[r0:r0 + c, :]
            qt = q[r0:r0 + c, :]
            diag = jnp.zeros((c, C), F32)
            for s in range(c):
                e = jnp.exp2(bt - b[r0 + s:r0 + s + 1, :])
                col = jnp.sum(qt * e * k[r0 + s:r0 + s + 1, :], axis=-1, keepdims=True)
                diag = jnp.where(sub_col == r0 + s, col, diag)
            diag = jnp.where(sub_col - r0 <= sub_row, diag, 0.0)
            if t == 0:
                row_blocks.append(diag)
                continue
            r_t = b[r0 - 1:r0, :]
            if k_run:
                step = jnp.exp2(r_t - b[r0 - c - 1:r0 - c, :])
                k_run = [kb * step for kb in k_run]
            k_run.append(k[r0 - c:r0, :] * jnp.exp2(r_t - b[r0 - c:r0, :]))
            k_t = jnp.concatenate(k_run + [jnp.zeros((C - r0, C), F32)], axis=0).astype(BF16)
            q_t = (qt * jnp.exp2(bt - r_t)).astype(BF16)
            off = lax.dot_general(q_t, k_t, _NT, preferred_element_type=F32)
            row_blocks.append(off + diag)
        scores = jnp.concatenate(row_blocks, axis=0)
        o = o + jnp.dot(scores.astype(BF16), v16, preferred_element_type=F32)

        k_dec = (k * jnp.exp2(b_last - b)).astype(BF16)
        state_ref[...] = (state_t * jnp.exp2(b_last)
                          + lax.dot_general(v16, k_dec, _TN, preferred_element_type=F32))

        r = lax.rsqrt(jnp.mean(o * o, axis=-1, keepdims=True) + EPS)
        o_ref[sl, :] = ((o * r) * on_ref[...] * _silu(g_ref[sl, :].astype(F32))).astype(BF16)


def _hgrn(proj, hgf, lb, onorm_g, batch, seq):
    rows = min(512, seq)
    nr = seq // rows
    cb = lambda base: (lambda b, h, r: (b * nr + r, base // HG_DIM + h))
    return pl.pallas_call(
        functools.partial(_hgrn_kernel, rows=rows),
        grid=(batch, HG_HEADS, nr),
        in_specs=[pl.BlockSpec((rows, HG_DIM), cb(COL_HG_Q)),
                  pl.BlockSpec((rows, HG_DIM), cb(0)),
                  pl.BlockSpec((rows, HG_DIM), cb(COL_HG_I)),
                  pl.BlockSpec((rows, HG_DIM), cb(COL_HG_G)),
                  pl.BlockSpec((1, HG_DIM), lambda b, h, r: (0, h)),
                  pl.BlockSpec((1, HG_DIM), lambda b, h, r: (0, 0))],
        out_specs=pl.BlockSpec((rows, HG_DIM), lambda b, h, r: (b * nr + r, h)),
        out_shape=jax.ShapeDtypeStruct((batch * seq, BRANCH_WIDTH), BF16),
        scratch_shapes=[pltpu.VMEM((HG_DIM, HG_DIM), F32)],
        compiler_params=_cparams("parallel", "parallel", "arbitrary"),
        name="hgrn2_mixer",
    )(proj, hgf, proj, proj, lb, onorm_g)


def _rope_tables(positions):
    pos = positions.astype(F32)[..., None]

    def tables(dim, reps):
        inv = ROPE_THETA ** (-jnp.arange(0, dim, 2, dtype=F32) / dim)
        ang = pos * inv
        c, s = jnp.cos(ang), jnp.sin(ang)
        return (jnp.concatenate([c, c] * reps, axis=-1),
                jnp.concatenate([-s, s] * reps, axis=-1))

    return tables(AT_DIM, 1) + tables(IDX_DIM, LANES // IDX_DIM)


def _rope_full(x, cos, sin_signed):
    return x * cos + pltpu.roll(x, AT_DIM // 2, 1) * sin_signed


def _rope_idx(x, cos, sin_signed, first_half):
    h = IDX_DIM // 2
    partner = jnp.where(first_half, pltpu.roll(x, LANES - h, 1), pltpu.roll(x, h, 1))
    return x * cos + partner * sin_signed


def _col_reduce(x, op):
    part = 64
    if x.shape[0] > part:
        x = op(x.reshape(x.shape[0] // part, part, x.shape[1]), axis=0)
    return op(x, axis=0, keepdims=True)


def _dsa_block(width, j, q_ref, g_refs, iq_refs, w_rows, ca, sa, ci, si, qn_ref, o_ref,
               k_s, vt_s, ik_s, key_s, key16_s, val_s, *, topk, first_half, low_group):
    QB = Q_BLOCK
    kk_s, ikk_s = k_s.at[0:width, :], ik_s.at[0:width, :]
    key_w, val_w = key_s.at[0:width, :], val_s.at[0:width, :]

    isc = jnp.zeros((width, QB), F32)
    per_slab = LANES // IDX_DIM
    for m in range(IDX_HEADS // per_slab):
        iq_ref = iq_refs[m // 2]
        off = (m % 2) * LANES
        xr = _rope_idx(iq_ref[:, off:off + LANES].astype(F32), ci, si, first_half)
        for par in range(per_slab):
            h = m * per_slab + par
            xm = jnp.where(low_group if par == 0 else jnp.logical_not(low_group), xr, 0.0)
            rel = lax.dot_general(ikk_s[...], xm.astype(BF16), _NT, preferred_element_type=F32)
            isc = isc + jnp.maximum(rel, 0.0) * w_rows[IDX_DIM + h:IDX_DIM + h + 1, :]
    qpos = j * QB + lax.broadcasted_iota(jnp.int32, (1, QB), 1)
    kpos = lax.broadcasted_iota(jnp.int32, (width, 1), 0)
    causal = kpos <= qpos
    sc = jnp.where(causal, isc, NEG_BIG)
    key_w[...] = sc

    kf = float(topk)
    if width <= topk:
        val_w[...] = jnp.where(causal, 1.0, 0.0)
    else:
        key16_w = key16_s.at[0:width, :]
        key16_w[...] = sc.astype(BF16)
        one_b, zero_b = jnp.ones((), BF16), jnp.zeros((), BF16)

        def as_score(image):
            bits = jnp.where(image < 0, image ^ jnp.int32(0x7FFFFFFF), image)
            return pltpu.bitcast(bits, F32)

        def count16(cand16):
            cut = as_score(cand16 << 16).astype(BF16)
            hit = jnp.where(key16_w[...] >= cut, one_b, zero_b)
            parts = [hit[r * LANES:(r + 1) * LANES, :] for r in range(width // LANES)]
            while len(parts) > 1:
                parts = [a + b for a, b in zip(parts[0::2], parts[1::2])] + parts[len(parts) & ~1:]
            return jnp.sum(parts[0].astype(F32), axis=0, keepdims=True)

        zero = jnp.zeros((1, QB), jnp.int32)
        lead0 = jnp.where(count16(zero) >= kf, zero, jnp.full((1, QB), -2**15, jnp.int32))

        def bisect16(i, lead):
            cand = lead + (jnp.int32(1) << (14 - i))
            return jnp.where(count16(cand) >= kf, cand, lead)

        lead = lax.fori_loop(0, 15, bisect16, lead0)
        base = jnp.maximum(lead << 16, jnp.int32(-2**31 + 2**15)) - jnp.int32(2**15)

        def count_ge(cand):
            return _col_reduce(jnp.where(key_w[...] >= as_score(cand), 1.0, 0.0), jnp.sum)

        def bisect(i, off):
            cand = off + (jnp.int32(1) << (16 - i))
            return jnp.where(count_ge(base + cand) >= kf, cand, off)

        thr = as_score(base + lax.fori_loop(0, 17, bisect, zero))

        key = key_w[...]
        gt = key > thr
        eq = key == thr
        need = kf - _col_reduce(jnp.where(gt, 1.0, 0.0), jnp.sum)
        rr = lax.broadcasted_iota(jnp.int32, (LANES, LANES), 0)
        cc = lax.broadcasted_iota(jnp.int32, (LANES, LANES), 1)
        lower = jnp.where(cc <= rr, 1.0, 0.0).astype(BF16)
        seen = jnp.zeros((1, QB), F32)
        for cidx in range(width // LANES):
            sl = slice(cidx * LANES, (cidx + 1) * LANES)
            eq_c = jnp.where(eq[sl, :], 1.0, 0.0)
            rank = jnp.dot(lower, eq_c.astype(BF16), preferred_element_type=F32) + seen
            take = jnp.where(gt[sl, :], 1.0, jnp.where(rank <= need, eq_c, 0.0))
            val_w[sl, :] = jnp.where(causal[sl, :], take, 0.0)
            seen = rank[LANES - 1:LANES, :]

    scale = math.log2(math.e) / math.sqrt(AT_DIM)
    valid = val_w[...] > 0.0
    for h in range(AT_HEADS):
        hs = slice(h * AT_DIM, (h + 1) * AT_DIM)
        qh = q_ref[:, hs].astype(F32)
        r = lax.rsqrt(jnp.mean(qh * qh, axis=-1, keepdims=True) + EPS)
        qh = _rope_full((qh * r) * qn_ref[...], ca, sa) * scale
        logits = lax.dot_general(kk_s[...], qh.astype(BF16), _NT, preferred_element_type=F32)
        logits = jnp.where(valid, logits, NEG_BIG)
        p = jnp.exp2(logits - _col_reduce(logits, jnp.max))
        denom = _col_reduce(p, jnp.sum)
        oh_t = jnp.dot(vt_s[:, 0:width], p.astype(BF16), preferred_element_type=F32) / denom
        g_ref = g_refs[h // 2]
        goff = (h % 2) * AT_DIM
        o_ref[:, hs] = (oh_t.T * _silu(g_ref[:, goff:goff + AT_DIM].astype(F32))).astype(BF16)


def _dsa_kernel(q_ref, g0_ref, g1_ref, iq0_ref, iq1_ref, k_ref, v_ref, ikw_ref,
                ca_ref, sa_ref, ci_ref, si_ref, qn_ref, kn_ref, o_ref,
                k_s, vt_s, ik_s, key_s, key16_s, val_s, *, seq, topk):
    j = pl.program_id(1)
    QB = Q_BLOCK
    lane = lax.broadcasted_iota(jnp.int32, (1, LANES), 1)
    first_half = (lane % IDX_DIM) < (IDX_DIM // 2)
    low_group = lane < IDX_DIM

    @pl.when(j == 0)
    def _():
        kk = k_ref[...].astype(F32)
        r = lax.rsqrt(jnp.mean(kk * kk, axis=-1, keepdims=True) + EPS)
        kk = (kk * r) * kn_ref[...]
        k_s[...] = _rope_full(kk, ca_ref[0], sa_ref[0]).astype(BF16)
        vt_s[...] = v_ref[...].astype(F32).T.astype(BF16)
        ik = _rope_idx(ikw_ref[...].astype(F32), ci_ref[0], si_ref[0], first_half)
        ik_s[...] = jnp.where(low_group, ik, pltpu.roll(ik, IDX_DIM, 1)).astype(BF16)

    rows = pl.ds(pl.multiple_of(j * QB, QB), QB)
    ca, sa = ca_ref[0, rows, :], sa_ref[0, rows, :]
    ci, si = ci_ref[0, rows, :], si_ref[0, rows, :]
    w_rows = ikw_ref[rows, :].astype(F32).T

    group = min(KEY_GROUP, seq)
    per_group = group // QB
    for n in range(1, seq // group + 1):
        @pl.when(j // per_group + 1 == n)
        def _(n=n):
            _dsa_block(n * group, j, q_ref, (g0_ref, g1_ref), (iq0_ref, iq1_ref), w_rows,
                       ca, sa, ci, si, qn_ref, o_ref, k_s, vt_s, ik_s, key_s, key16_s, val_s,
                       topk=topk, first_half=first_half, low_group=low_group)


def _dsa(proj, tables, qn_g, kn_g, batch, seq):
    nb = seq // Q_BLOCK
    topk = min(TOPK_MAX, seq // 4)
    ca, sa, ci, si = tables
    half = BRANCH_WIDTH // 2
    qrow = lambda width, col: pl.BlockSpec((Q_BLOCK, width), lambda b, j: (b * nb + j, col // width))
    krow = lambda col: pl.BlockSpec((seq, LANES), lambda b, j: (b, col // LANES))
    tab = pl.BlockSpec((1, seq, LANES), lambda b, j: (b, 0, 0))
    vec = pl.BlockSpec((1, LANES), lambda b, j: (0, 0))
    return pl.pallas_call(
        functools.partial(_dsa_kernel, seq=seq, topk=topk),
        grid=(batch, nb),
        in_specs=[qrow(BRANCH_WIDTH, COL_AT_Q),
                  qrow(half, COL_AT_G), qrow(half, COL_AT_G + half),
                  qrow(half, COL_IX_Q), qrow(half, COL_IX_Q + half),
                  krow(COL_AT_K), krow(COL_AT_V), krow(COL_IX_KW),
                  tab, tab, tab, tab, vec, vec],
        out_specs=pl.BlockSpec((Q_BLOCK, BRANCH_WIDTH), lambda b, j: (b * nb + j, 0)),
        out_shape=jax.ShapeDtypeStruct((batch * seq, BRANCH_WIDTH), BF16),
        scratch_shapes=[pltpu.VMEM((seq, LANES), BF16), pltpu.VMEM((LANES, seq), BF16),
                        pltpu.VMEM((seq, LANES), BF16),
                        pltpu.VMEM((seq, Q_BLOCK), F32), pltpu.VMEM((seq, Q_BLOCK), BF16),
                        pltpu.VMEM((seq, Q_BLOCK), F32)],
        compiler_params=_cparams("parallel", "arbitrary"),
        name="dsa_mixer",
    )(proj, proj, proj, proj, proj, proj, proj, proj, ca, sa, ci, si, qn_g, kn_g)


def _s5prep_kernel(are_ref, aim_ref, ldt_ref, btr_ref, bti_ref, cr_ref, ci_ref, perm_ref,
                   t_ref, min_ref, nout_ref, al_ref):
    Lb, G16 = S5_BLOCK, S5_GROUP
    hp = lax.Precision.HIGHEST
    a_re, a_im = are_ref[0], aim_ref[0]
    dt = jnp.exp(ldt_ref[0])
    mag = jnp.exp(a_re * dt)
    ang = a_im * dt
    ab_r, ab_i = mag * jnp.cos(ang), mag * jnp.sin(ang)
    nr, ni = ab_r - 1.0, ab_i
    den = a_re * a_re + a_im * a_im
    fr = (nr * a_re + ni * a_im) / den
    fi = (ni * a_re - nr * a_im) / den
    bt_r, bt_i = btr_ref[0], bti_ref[0]
    bb_r = fr * bt_r - fi * bt_i
    bb_i = fr * bt_i + fi * bt_r
    c_r, c_i = cr_ref[0], ci_ref[0]

    pw_r, pw_i = [jnp.ones_like(ab_r)], [jnp.zeros_like(ab_r)]
    for _ in range(Lb):
        pr, pi = pw_r[-1], pw_i[-1]
        pw_r.append(pr * ab_r - pi * ab_i)
        pw_i.append(pr * ab_i + pi * ab_r)

    def readout(first):
        re = jnp.concatenate([c_r * pw_r[first + t] - c_i * pw_i[first + t] for t in range(Lb)], axis=0)
        im = jnp.concatenate([c_r * pw_i[first + t] + c_i * pw_r[first + t] for t in range(Lb)], axis=0)
        return re, im

    wc_r, wc_i = readout(0)
    taps = (lax.dot_general(bb_r, wc_r, _NT, precision=hp, preferred_element_type=F32)
            - lax.dot_general(bb_i, wc_i, _NT, precision=hp, preferred_element_type=F32))
    t_mat = jnp.concatenate(
        [taps] + [jnp.concatenate([jnp.zeros((G16, s * G16), F32), taps[:, :(Lb - s) * G16]], axis=1)
                  for s in range(1, Lb)], axis=0)
    m_r = jnp.concatenate([pw_r[Lb - 1 - s] * bb_r - pw_i[Lb - 1 - s] * bb_i for s in range(Lb)], axis=0)
    m_i = jnp.concatenate([pw_r[Lb - 1 - s] * bb_i + pw_i[Lb - 1 - s] * bb_r for s in range(Lb)], axis=0)
    m_in = jnp.concatenate([m_r, m_i, m_i, m_r], axis=1)
    n_r, n_i = readout(1)
    n_out = jnp.concatenate([n_r, -n_i], axis=1)

    perm = perm_ref[0]
    t_rows = jnp.dot(perm, t_mat.astype(BF16), preferred_element_type=F32).astype(BF16)
    t_ref[0] = lax.dot_general(t_rows, perm, _NT, preferred_element_type=F32).astype(BF16)
    min_ref[0] = jnp.dot(perm, m_in.astype(BF16), preferred_element_type=F32).astype(BF16)
    nout_ref[0] = jnp.dot(perm, n_out.astype(BF16), preferred_element_type=F32).astype(BF16)
    al_r, al_i = pw_r[Lb], pw_i[Lb]
    al_ref[0] = jnp.concatenate([jnp.concatenate([al_r, al_r], axis=1),
                                 jnp.concatenate([-al_i, al_i], axis=1),
                                 jnp.zeros((6, 2 * S5_STATE), F32)], axis=0)


def _s5_prepare(a_re, a_im, log_dt, b_re, b_im, c_re, c_im):
    G, P, K = S5_GROUPS, S5_STATE, S5_BLOCK * S5_GROUP
    half = LANES // S5_GROUP
    g3 = lambda shape: pl.BlockSpec((1,) + shape, lambda g: (g, 0, 0))
    return pl.pallas_call(
        _s5prep_kernel,
        grid=(G,),
        in_specs=[g3((1, P)), g3((1, P)), g3((1, 1)), g3((S5_GROUP, P)), g3((S5_GROUP, P)),
                  g3((S5_GROUP, P)), g3((S5_GROUP, P)),
                  pl.BlockSpec((1, K, K), lambda g: (g % half, 0, 0))],
        out_specs=[g3((K, K)), g3((K, 4 * P)), g3((K, 2 * P)), g3((8, 2 * P))],
        out_shape=[jax.ShapeDtypeStruct((G, K, K), BF16), jax.ShapeDtypeStruct((G, K, 4 * P), BF16),
                   jax.ShapeDtypeStruct((G, K, 2 * P), BF16), jax.ShapeDtypeStruct((G, 8, 2 * P), F32)],
        compiler_params=_cparams("parallel"),
        name="s5_prepare",
    )(a_re.reshape(G, 1, P), a_im.reshape(G, 1, P), log_dt.reshape(G, 1, 1),
      jnp.swapaxes(b_re, 1, 2), jnp.swapaxes(b_im, 1, 2), c_re, c_im, jnp.asarray(_s5_slot_perm(), BF16))


def _s5_slot(step, g):
    half = LANES // S5_GROUP
    return (step // half) * half + (g % half + step % half) % half


def _s5_slot_perm():
    half = LANES // S5_GROUP
    perm = np.zeros((half, S5_BLOCK * S5_GROUP, S5_BLOCK * S5_GROUP), np.float32)
    for g in range(half):
        for step in range(S5_BLOCK):
            for ch in range(S5_GROUP):
                perm[g, _s5_slot(step, g) * S5_GROUP + ch, step * S5_GROUP + ch] = 1.0
    return perm


def _s5_in_kernel(*refs, nblk):
    u_refs, o_ref = refs[:-1], refs[-1]
    half = LANES // S5_GROUP
    for k, u_ref in enumerate(u_refs):
        for step in range(S5_BLOCK):
            slab = u_ref[pl.ds(step, nblk, stride=S5_BLOCK), :]
            shift = (step % half) * S5_GROUP
            if shift:
                slab = pltpu.roll(slab, shift, 1)
            slab = slab.astype(BF16)
            for gi in range(half):
                g = k * half + gi
                lo = (_s5_slot(step, g) % half) * S5_GROUP
                base = (step // half) * LANES
                o_ref[g, :, base + lo:base + lo + S5_GROUP] = slab[:, lo:lo + S5_GROUP]


def _s5_out_kernel(*refs, nblk):
    y_ref, o_refs = refs[0], refs[1:]
    half = LANES // S5_GROUP
    slot_of_lane = lax.broadcasted_iota(jnp.int32, (1, LANES), 1) // S5_GROUP
    for k, o_ref in enumerate(o_refs):
        for step in range(S5_BLOCK):
            base = (step // half) * LANES
            slab = None
            for gi in range(half):
                g = k * half + gi
                src = y_ref[g, :, base:base + LANES]
                slab = src if slab is None else jnp.where(slot_of_lane == _s5_slot(step, g) % half, src, slab)
            slab = slab.astype(F32)
            shift = (step % half) * S5_GROUP
            if shift:
                slab = pltpu.roll(slab, LANES - shift, 1)
            o_ref[pl.ds(step, nblk, stride=S5_BLOCK), :] = slab


def _s5_kernel(u_ref, t_ref, min_ref, nout_ref, al_ref, y_ref, inj_s, injx_s, xp_s, *, batch, nblk):
    P2 = 2 * S5_STATE
    u = u_ref[0]
    inj = jnp.dot(u, min_ref[0], preferred_element_type=F32)
    inj_s[...] = inj[:, 0:P2]
    injx_s[...] = inj[:, P2:2 * P2]
    a_same = al_ref[0, 0:1, :]
    a_cross = al_ref[0, 1:2, :]

    def step(jb, carry):
        x, xs = carry
        rows = pl.ds(jb, batch, stride=nblk)
        xp_s[rows, :] = x
        return (x * a_same + xs * a_cross + inj_s[rows, :],
                xs * a_same - x * a_cross + injx_s[rows, :])

    z = jnp.zeros((batch, P2), F32)
    lax.fori_loop(0, nblk, step, (z, z), unroll=4)
    y = jnp.dot(u, t_ref[0], preferred_element_type=F32)
    y = y + lax.dot_general(xp_s[...].astype(BF16), nout_ref[0], _NT, preferred_element_type=F32)
    y_ref[0] = y.astype(BF16)


def _s5_scan(u32, ops, batch, seq):
    t_mat, m_in, n_out, al = ops
    G, P, K = S5_GROUPS, S5_STATE, S5_BLOCK * S5_GROUP
    nblk = seq // S5_BLOCK
    R = nblk * batch
    nslab = BRANCH_WIDTH // LANES
    slab = lambda k: pl.BlockSpec((seq, LANES), lambda b, k=k: (b, k))
    grouped = pl.BlockSpec((G, nblk, K), lambda b: (0, b, 0))
    u = pl.pallas_call(
        functools.partial(_s5_in_kernel, nblk=nblk),
        grid=(batch,),
        in_specs=[slab(k) for k in range(nslab)],
        out_specs=grouped,
        out_shape=jax.ShapeDtypeStruct((G, R, K), BF16),
        compiler_params=_cparams("parallel"),
        name="s5_relayout_in",
    )(*([u32] * nslab))
    g3 = lambda shape: pl.BlockSpec((1,) + shape, lambda g: (g, 0, 0))
    y = pl.pallas_call(
        functools.partial(_s5_kernel, batch=batch, nblk=nblk),
        grid=(G,),
        in_specs=[g3((R, K)), g3((K, K)), g3((K, 4 * P)), g3((K, 2 * P)), g3((8, 2 * P))],
        out_specs=g3((R, K)),
        out_shape=jax.ShapeDtypeStruct((G, R, K), BF16),
        scratch_shapes=[pltpu.VMEM((R, 2 * P), F32), pltpu.VMEM((R, 2 * P), F32),
                        pltpu.VMEM((R, 2 * P), F32)],
        compiler_params=_cparams("parallel"),
        name="s5_scan",
    )(u, t_mat, m_in, n_out, al)
    return pl.pallas_call(
        functools.partial(_s5_out_kernel, nblk=nblk),
        grid=(batch,),
        in_specs=[grouped],
        out_specs=[pl.BlockSpec((seq, LANES), lambda b: (b, 0)) for _ in range(nslab)],
        out_shape=[jax.ShapeDtypeStruct((batch * seq, LANES), F32) for _ in range(nslab)],
        compiler_params=_cparams("parallel"),
        name="s5_relayout_out",
    )(y)


def _merge_kernel(x_ref, gate_ref, ya_ref, yb_ref, y50_ref, y51_ref, y52_ref, y53_ref, u_ref, sg_ref,
                  m0_ref, m1_ref, m2_ref, dsk_ref, gw_ref, gb_ref, wb_ref, wo_ref, o_ref):
    y5 = jnp.concatenate([y50_ref[...], y51_ref[...], y52_ref[...], y53_ref[...]], axis=1)
    y = y5 + dsk_ref[...] * u_ref[...].astype(F32)
    y = jax.nn.gelu(y)
    glu = jnp.dot(y.astype(BF16), gw_ref[...], preferred_element_type=F32) + gb_ref[...]
    yc = (y * _sigmoid(glu) * _silu(sg_ref[...].astype(F32))).astype(BF16)
    merged = jnp.zeros(x_ref.shape, F32)
    for n, (yn, m_ref) in enumerate(((ya_ref[...], m0_ref), (yb_ref[...], m1_ref), (yc, m2_ref))):
        yd = jnp.dot(yn, wb_ref[n], preferred_element_type=F32)
        merged = merged + _sigmoid(m_ref[...].astype(F32)) * yd
    o_ref[...] = x_ref[...] + gate_ref[0] * jnp.dot(merged.astype(BF16), wo_ref[...],
                                                    preferred_element_type=F32)


def _merge(x2, gate, ya, yb, y5, proj, d_skip, glu_w, glu_b, w_branch, w_out, seq):
    M, D = x2.shape
    W = BRANCH_WIDTH
    tm = min(512, seq)
    per_batch = seq // tm
    row = lambda cols, colblk: pl.BlockSpec((tm, cols), lambda i: (i, colblk))
    full = lambda shape: pl.BlockSpec(shape, lambda i: (0,) * len(shape))
    return pl.pallas_call(
        _merge_kernel,
        grid=(M // tm,),
        in_specs=[row(D, 0),
                  pl.BlockSpec((1, 1, D), lambda i: (i // per_batch, 0, 0)),
                  row(W, 0), row(W, 0),
                  row(LANES, 0), row(LANES, 0), row(LANES, 0), row(LANES, 0),
                  row(W, COL_S5_U // W), row(W, COL_S5_G // W),
                  row(D, COL_MERGE // D), row(D, COL_MERGE // D + 1), row(D, COL_MERGE // D + 2),
                  full((1, W)), full((W, W)), full((1, W)), full((N_BRANCH, W, D)), full((D, D))],
        out_specs=row(D, 0),
        out_shape=jax.ShapeDtypeStruct((M, D), F32),
        compiler_params=_cparams("parallel"),
        name="merge_out",
    )(x2, gate, ya, yb, *y5, proj, proj, proj, proj, proj, d_skip, glu_w, glu_b, w_branch, w_out)


def kernel(x, c, positions, ada_w, ada_b, norm_g, w_in, hg_lb_logits, hg_onorm_g, at_qnorm_g,
           at_knorm_g, s5_a_re, s5_a_im, s5_log_dt, s5_b_re, s5_b_im, s5_c_re, s5_c_im, s5_d,
           s5_glu_w, s5_glu_b, w_branch, w_out):
    B, S, D = x.shape
    L = ada_w.shape[0]
    lb_all = _lower_bounds(hg_lb_logits.astype(F32))
    mod = _modulation(c, ada_w, ada_b)
    w_all = _permute_w_in(w_in)
    tables = _rope_tables(positions)
    x2 = x.reshape(B * S, D)
    for l in range(L):
        shift = mod[l, :, None, 0:D]
        scale = mod[l, :, None, D:2 * D]
        gate = mod[l, :, None, 2 * D:3 * D]
        proj, hgf, u32 = _projection(x2, norm_g[l][None, :], shift, scale, w_all, l, S)
        ya = _hgrn(proj, hgf, lb_all[l][None, :], hg_onorm_g[l][None, :], B, S)
        yb = _dsa(proj, tables, at_qnorm_g[l][None, :], at_knorm_g[l][None, :], B, S)
        ops = _s5_prepare(s5_a_re[l], s5_a_im[l], s5_log_dt[l], s5_b_re[l], s5_b_im[l],
                          s5_c_re[l], s5_c_im[l])
        y5 = _s5_scan(u32, ops, B, S)
        x2 = _merge(x2, gate, ya, yb, y5, proj, s5_d[l][None, :], s5_glu_w[l].astype(BF16),
                    s5_glu_b[l][None, :], w_branch[l].astype(BF16), w_out[l].astype(BF16), S)
    return x2.reshape(B, S, D)
```

```python
import functools
import math

import jax
import jax.numpy as jnp
import numpy as np
from jax import lax
from jax.experimental import pallas as pl
from jax.experimental.pallas import tpu as pltpu

F32 = jnp.float32
BF16 = jnp.bfloat16

D_MODEL = 1024
DEPTH = 4
BRANCH_WIDTH = 512
N_BRANCH = 3
EPS = 1e-6
NEG_BIG = -1e30
HG_HEADS = 4
HG_DIM = 128
AT_HEADS = 4
AT_DIM = 128
IDX_HEADS = 8
IDX_DIM = 64
TOPK_MAX = 256
Q_BLOCK = 128
ROPE_THETA = 10000.0
S5_GROUP = 16
S5_GROUPS = BRANCH_WIDTH // S5_GROUP
S5_STATE = 64
N_IN = 8008

LANES = 128
HG_CHUNK = 128
HG_SUB = 8
S5_BLOCK = 16
KEY_GROUP = 256
VMEM_LIMIT = 52 * 1024 * 1024

REF_ALIGNED = 3840
REF_TAIL = 3912
COL_S5_U = 0
COL_S5_G = 512
COL_MERGE = 1024
COL_HG_Q = 4096
COL_HG_F = 4608
COL_HG_I = 5120
COL_HG_G = 5632
COL_AT_Q = 6144
COL_AT_K = 6656
COL_AT_V = 6784
COL_AT_G = 6912
COL_IX_Q = 7424
COL_IX_KW = 7936
N_PROJ = 8192
PROJ_TN = 2048
assert COL_HG_F % PROJ_TN + BRANCH_WIDTH <= PROJ_TN and COL_S5_U % PROJ_TN + BRANCH_WIDTH <= PROJ_TN

_NT = (((1,), (1,)), ((), ()))
_TN = (((0,), (0,)), ((), ()))


def _cparams(*sem):
    return pltpu.CompilerParams(dimension_semantics=sem, vmem_limit_bytes=VMEM_LIMIT)


def _sigmoid(x):
    return 1.0 / (1.0 + jnp.exp(-x))


def _silu(x):
    return x * _sigmoid(x)


def _lb_kernel(z_ref, o_ref):
    z = z_ref[...]
    e = jnp.exp(z - jnp.max(z, axis=0, keepdims=True))
    p = e / jnp.sum(e, axis=0, keepdims=True)
    acc = jnp.zeros_like(p[0:1])
    for l in range(z.shape[0]):
        acc = acc + p[l:l + 1]
        o_ref[l:l + 1, :] = acc - p[0:1]


def _lower_bounds(logits):
    return pl.pallas_call(
        _lb_kernel, out_shape=jax.ShapeDtypeStruct(logits.shape, F32), name="hg_lower_bounds",
    )(logits)


def _mod_kernel(c_ref, w_ref, b_ref, o_ref):
    c = c_ref[...]
    o_ref[0] = jnp.dot(_silu(c), w_ref[0], precision=lax.Precision.HIGHEST,
                       preferred_element_type=F32) + b_ref[0]


def _modulation(c, ada_w, ada_b):
    L, D, N = ada_w.shape
    B = c.shape[0]
    tn = 512
    return pl.pallas_call(
        _mod_kernel,
        grid=(L, N // tn),
        in_specs=[pl.BlockSpec((B, D), lambda l, n: (0, 0)),
                  pl.BlockSpec((1, D, tn), lambda l, n: (l, 0, n)),
                  pl.BlockSpec((1, 1, tn), lambda l, n: (l, 0, n))],
        out_specs=pl.BlockSpec((1, B, tn), lambda l, n: (l, 0, n)),
        out_shape=jax.ShapeDtypeStruct((L, B, N), F32),
        compiler_params=_cparams("parallel", "parallel"),
        name="adaln_modulation",
    )(c, ada_w, ada_b.reshape(L, 1, N))


def _wprep_kernel(w_ref, o_ref):
    tail = N_IN - REF_TAIL
    o_ref[0, :, 0:tail] = w_ref[0, :, REF_TAIL:N_IN].astype(BF16)
    o_ref[0, :, tail:tail + REF_ALIGNED] = w_ref[0, :, 0:REF_ALIGNED].astype(BF16)
    kw = w_ref[0, :, REF_ALIGNED:REF_ALIGNED + LANES]
    lane = lax.broadcasted_iota(jnp.int32, kw.shape, 1)
    o_ref[0, :, COL_IX_KW:COL_IX_KW + LANES] = jnp.where(lane < REF_TAIL - REF_ALIGNED, kw, 0.0).astype(BF16)
    o_ref[0, :, COL_IX_KW + LANES:N_PROJ] = jnp.zeros((kw.shape[0], N_PROJ - COL_IX_KW - LANES), BF16)


def _permute_w_in(w_in):
    L, D, N = w_in.shape
    tr = 128
    return pl.pallas_call(
        _wprep_kernel,
        grid=(L, D // tr),
        in_specs=[pl.BlockSpec((1, tr, N), lambda l, r: (l, r, 0))],
        out_specs=pl.BlockSpec((1, tr, N_PROJ), lambda l, r: (l, r, 0)),
        out_shape=jax.ShapeDtypeStruct((L, D, N_PROJ), BF16),
        compiler_params=_cparams("parallel", "parallel"),
        name="w_in_prepare",
    )(w_in)


def _proj_kernel(x_ref, g_ref, shift_ref, scale_ref, w_ref, o_ref, f_ref, u_ref, h_ref):
    j = pl.program_id(1)

    @pl.when(j == 0)
    def _():
        x = x_ref[...]
        r = lax.rsqrt(jnp.mean(x * x, axis=-1, keepdims=True) + EPS)
        h = (x * r) * g_ref[...] * (1.0 + scale_ref[0]) + shift_ref[0]
        h_ref[...] = h.astype(BF16)

    acc = jnp.dot(h_ref[...], w_ref[0], preferred_element_type=F32)
    o_ref[...] = acc.astype(BF16)

    @pl.when(j == COL_HG_F // PROJ_TN)
    def _():
        off = COL_HG_F % PROJ_TN
        f_ref[...] = acc[:, off:off + BRANCH_WIDTH]

    @pl.when(j == COL_S5_U // PROJ_TN)
    def _():
        off = COL_S5_U % PROJ_TN
        u_ref[...] = acc[:, off:off + BRANCH_WIDTH]


def _projection(x2, norm_g, shift, scale, w_all, layer, seq):
    M, D = x2.shape
    tm = min(1024, seq)
    per_batch = seq // tm
    return pl.pallas_call(
        _proj_kernel,
        grid=(M // tm, N_PROJ // PROJ_TN),
        in_specs=[pl.BlockSpec((tm, D), lambda i, j: (i, 0)),
                  pl.BlockSpec((1, D), lambda i, j: (0, 0)),
                  pl.BlockSpec((1, 1, D), lambda i, j: (i // per_batch, 0, 0)),
                  pl.BlockSpec((1, 1, D), lambda i, j: (i // per_batch, 0, 0)),
                  pl.BlockSpec((1, D, PROJ_TN), lambda i, j: (layer, 0, j))],
        out_specs=[pl.BlockSpec((tm, PROJ_TN), lambda i, j: (i, j)),
                   pl.BlockSpec((tm, BRANCH_WIDTH), lambda i, j: (i, 0)),
                   pl.BlockSpec((tm, BRANCH_WIDTH), lambda i, j: (i, 0))],
        out_shape=[jax.ShapeDtypeStruct((M, N_PROJ), BF16),
                   jax.ShapeDtypeStruct((M, BRANCH_WIDTH), F32),
                   jax.ShapeDtypeStruct((M, BRANCH_WIDTH), F32)],
        scratch_shapes=[pltpu.VMEM((tm, D), BF16)],
        compiler_params=_cparams("parallel", "arbitrary"),
        name="norm_in_proj",
    )(x2, norm_g, shift, scale, w_all)


def _split3(x):
    hi = x.astype(BF16)
    r1 = x - hi.astype(F32)
    mid = r1.astype(BF16)
    lo = (r1 - mid.astype(F32)).astype(BF16)
    return hi, mid, lo


def _hgrn_kernel(q_ref, f_ref, i_ref, g_ref, lb_ref, on_ref, o_ref, state_ref, *, rows):
    C, c = HG_CHUNK, HG_SUB
    nsub = C // c

    @pl.when(pl.program_id(2) == 0)
    def _():
        state_ref[...] = jnp.zeros_like(state_ref)

    lb = lb_ref[...]
    row_id = lax.broadcasted_iota(jnp.int32, (C, C), 0)
    col_id = lax.broadcasted_iota(jnp.int32, (C, C), 1)
    tril = jnp.where(col_id <= row_id, 1.0, 0.0).astype(BF16)
    sub_row = lax.broadcasted_iota(jnp.int32, (c, C), 0)
    sub_col = lax.broadcasted_iota(jnp.int32, (c, C), 1)

    for ci in range(rows // C):
        sl = slice(ci * C, (ci + 1) * C)
        q = q_ref[sl, :].astype(F32)
        v16 = i_ref[sl, :]
        fg = lb + (1.0 - lb) * _sigmoid(f_ref[sl, :])
        lf = jnp.log2(jnp.maximum(fg, 1e-30))
        k = 1.0 - fg
        hi, mid, lo = _split3(lf)
        b = (jnp.dot(tril, hi, preferred_element_type=F32)
             + jnp.dot(tril, mid, preferred_element_type=F32)
             + jnp.dot(tril, lo, preferred_element_type=F32))
        b_last = b[C - 1:C, :]

        state_t = state_ref[...]
        q0 = (q * jnp.exp2(b)).astype(BF16)
        o = lax.dot_general(q0, state_t.astype(BF16), _NT, preferred_element_type=F32)

        row_blocks = []
        k_run = []
        for t in range(nsub):
            r0 = t * c
            bt = b[r0:r0 + c, :]
            qt = q[r0:r0 + c, :]
            diag = jnp.zeros((c, C), F32)
            for s in range(c):
                e = jnp.exp2(bt - b[r0 + s:r0 + s + 1, :])
                col = jnp.sum(qt * e * k[r0 + s:r0 + s + 1, :], axis=-1, keepdims=True)
                diag = jnp.where(sub_col == r0 + s, col, diag)
            diag = jnp.where(sub_col - r0 <= sub_row, diag, 0.0)
            if t == 0:
                row_blocks.append(diag)
                continue
            r_t = b[r0 - 1:r0, :]
            if k_run:
                step = jnp.exp2(r_t - b[r0 - c - 1:r0 - c, :])
                k_run = [kb * step for kb in k_run]
            k_run.append(k[r0 - c:r0, :] * jnp.exp2(r_t - b[r0 - c:r0, :]))
            k_t = jnp.concatenate(k_run + [jnp.zeros((C - r0, C), F32)], axis=0).astype(BF16)
            q_t = (qt * jnp.exp2(bt - r_t)).astype(BF16)
            off = lax.dot_general(q_t, k_t, _NT, preferred_element_type=F32)
            row_blocks.append(off + diag)
        scores = jnp.concatenate(row_blocks, axis=0)
        o = o + jnp.dot(scores.astype(BF16), v16, preferred_element_type=F32)

        k_dec = (k * jnp.exp2(b_last - b)).astype(BF16)
        state_ref[...] = (state_t * jnp.exp2(b_last)
                          + lax.dot_general(v16, k_dec, _TN, preferred_element_type=F32))

        r = lax.rsqrt(jnp.mean(o * o, axis=-1, keepdims=True) + EPS)
        o_ref[sl, :] = ((o * r) * on_ref[...] * _silu(g_ref[sl, :].astype(F32))).astype(BF16)


def _hgrn(proj, hgf, lb, onorm_g, batch, seq):
    rows = min(1024, seq)
    nr = seq // rows
    cb = lambda base: (lambda b, h, r: (b * nr + r, base // HG_DIM + h))
    return pl.pallas_call(
        functools.partial(_hgrn_kernel, rows=rows),
        grid=(batch, HG_HEADS, nr),
        in_specs=[pl.BlockSpec((rows, HG_DIM), cb(COL_HG_Q)),
                  pl.BlockSpec((rows, HG_DIM), cb(0)),
                  pl.BlockSpec((rows, HG_DIM), cb(COL_HG_I)),
                  pl.BlockSpec((rows, HG_DIM), cb(COL_HG_G)),
                  pl.BlockSpec((1, HG_DIM), lambda b, h, r: (0, h)),
                  pl.BlockSpec((1, HG_DIM), lambda b, h, r: (0, 0))],
        out_specs=pl.BlockSpec((rows, HG_DIM), lambda b, h, r: (b * nr + r, h)),
        out_shape=jax.ShapeDtypeStruct((batch * seq, BRANCH_WIDTH), BF16),
        scratch_shapes=[pltpu.VMEM((HG_DIM, HG_DIM), F32)],
        compiler_params=_cparams("parallel", "parallel", "arbitrary"),
        name="hgrn2_mixer",
    )(proj, hgf, proj, proj, lb, onorm_g)


def _rope_tables(positions):
    pos = positions.astype(F32)[..., None]

    def tables(dim, reps):
        inv = ROPE_THETA ** (-jnp.arange(0, dim, 2, dtype=F32) / dim)
        ang = pos * inv
        c, s = jnp.cos(ang), jnp.sin(ang)
        return (jnp.concatenate([c, c] * reps, axis=-1),
                jnp.concatenate([-s, s] * reps, axis=-1))

    return tables(AT_DIM, 1) + tables(IDX_DIM, LANES // IDX_DIM)


def _rope_full(x, cos, sin_signed):
    return x * cos + pltpu.roll(x, AT_DIM // 2, 1) * sin_signed


def _rope_idx(x, cos, sin_signed, first_half):
    h = IDX_DIM // 2
    partner = jnp.where(first_half, pltpu.roll(x, LANES - h, 1), pltpu.roll(x, h, 1))
    return x * cos + partner * sin_signed


def _col_reduce(x, op):
    part = 64
    if x.shape[0] > part:
        x = op(x.reshape(x.shape[0] // part, part, x.shape[1]), axis=0)
    return op(x, axis=0, keepdims=True)


def _dsa_block(width, j, q_ref, g_refs, iq_refs, w_rows, ca, sa, ci, si, qn_ref, o_ref,
               k_s, vt_s, ik_s, key_s, val_s, *, topk, first_half, low_group):
    QB = Q_BLOCK
    kk_s, ikk_s = k_s.at[0:width, :], ik_s.at[0:width, :]
    key_w, val_w = key_s.at[0:width, :], val_s.at[0:width, :]

    isc = jnp.zeros((width, QB), F32)
    per_slab = LANES // IDX_DIM
    for m in range(IDX_HEADS // per_slab):
        iq_ref = iq_refs[m // 2]
        off = (m % 2) * LANES
        xr = _rope_idx(iq_ref[:, off:off + LANES].astype(F32), ci, si, first_half)
        for par in range(per_slab):
            h = m * per_slab + par
            xm = jnp.where(low_group if par == 0 else jnp.logical_not(low_group), xr, 0.0)
            rel = lax.dot_general(ikk_s[...], xm.astype(BF16), _NT, preferred_element_type=F32)
            isc = isc + jnp.maximum(rel, 0.0) * w_rows[IDX_DIM + h:IDX_DIM + h + 1, :]
    qpos = j * QB + lax.broadcasted_iota(jnp.int32, (1, QB), 1)
    kpos = lax.broadcasted_iota(jnp.int32, (width, 1), 0)
    causal = kpos <= qpos
    key_w[...] = jnp.where(causal, isc, NEG_BIG)

    kf = float(topk)
    if width <= topk:
        val_w[...] = jnp.where(causal, 1.0, 0.0)
    else:
        def as_score(image):
            bits = jnp.where(image < 0, image ^ jnp.int32(0x7FFFFFFF), image)
            return pltpu.bitcast(bits, F32)

        def count_ge(cand):
            return _col_reduce(jnp.where(key_w[...] >= as_score(cand), 1.0, 0.0), jnp.sum)

        zero = jnp.zeros((1, QB), jnp.int32)
        ans0 = jnp.where(count_ge(zero) >= kf, zero, jnp.full((1, QB), -2**31, jnp.int32))

        def bisect(i, ans):
            cand = ans + (jnp.int32(1) << (30 - i))
            return jnp.where(count_ge(cand) >= kf, cand, ans)

        thr = as_score(lax.fori_loop(0, 31, bisect, ans0))

        key = key_w[...]
        gt = key > thr
        eq = key == thr
        need = kf - _col_reduce(jnp.where(gt, 1.0, 0.0), jnp.sum)
        rr = lax.broadcasted_iota(jnp.int32, (LANES, LANES), 0)
        cc = lax.broadcasted_iota(jnp.int32, (LANES, LANES), 1)
        lower = jnp.where(cc <= rr, 1.0, 0.0).astype(BF16)
        seen = jnp.zeros((1, QB), F32)
        for cidx in range(width // LANES):
            sl = slice(cidx * LANES, (cidx + 1) * LANES)
            eq_c = jnp.where(eq[sl, :], 1.0, 0.0)
            rank = jnp.dot(lower, eq_c.astype(BF16), preferred_element_type=F32) + seen
            take = jnp.where(gt[sl, :], 1.0, jnp.where(rank <= need, eq_c, 0.0))
            val_w[sl, :] = jnp.where(causal[sl, :], take, 0.0)
            seen = rank[LANES - 1:LANES, :]

    scale = math.log2(math.e) / math.sqrt(AT_DIM)
    valid = val_w[...] > 0.0
    for h in range(AT_HEADS):
        hs = slice(h * AT_DIM, (h + 1) * AT_DIM)
        qh = q_ref[:, hs].astype(F32)
        r = lax.rsqrt(jnp.mean(qh * qh, axis=-1, keepdims=True) + EPS)
        qh = _rope_full((qh * r) * qn_ref[...], ca, sa) * scale
        logits = lax.dot_general(kk_s[...], qh.astype(BF16), _NT, preferred_element_type=F32)
        logits = jnp.where(valid, logits, NEG_BIG)
        p = jnp.exp2(logits - _col_reduce(logits, jnp.max))
        denom = _col_reduce(p, jnp.sum)
        oh_t = jnp.dot(vt_s[:, 0:width], p.astype(BF16), preferred_element_type=F32) / denom
        g_ref = g_refs[h // 2]
        goff = (h % 2) * AT_DIM
        o_ref[:, hs] = (oh_t.T * _silu(g_ref[:, goff:goff + AT_DIM].astype(F32))).astype(BF16)


def _dsa_kernel(q_ref, g0_ref, g1_ref, iq0_ref, iq1_ref, k_ref, v_ref, ikw_ref,
                ca_ref, sa_ref, ci_ref, si_ref, qn_ref, kn_ref, o_ref,
                k_s, vt_s, ik_s, key_s, val_s, *, seq, topk):
    j = pl.program_id(1)
    QB = Q_BLOCK
    lane = lax.broadcasted_iota(jnp.int32, (1, LANES), 1)
    first_half = (lane % IDX_DIM) < (IDX_DIM // 2)
    low_group = lane < IDX_DIM

    @pl.when(j == 0)
    def _():
        kk = k_ref[...].astype(F32)
        r = lax.rsqrt(jnp.mean(kk * kk, axis=-1, keepdims=True) + EPS)
        kk = (kk * r) * kn_ref[...]
        k_s[...] = _rope_full(kk, ca_ref[0], sa_ref[0]).astype(BF16)
        vt_s[...] = v_ref[...].astype(F32).T.astype(BF16)
        ik = _rope_idx(ikw_ref[...].astype(F32), ci_ref[0], si_ref[0], first_half)
        ik_s[...] = jnp.where(low_group, ik, pltpu.roll(ik, IDX_DIM, 1)).astype(BF16)

    rows = pl.ds(pl.multiple_of(j * QB, QB), QB)
    ca, sa = ca_ref[0, rows, :], sa_ref[0, rows, :]
    ci, si = ci_ref[0, rows, :], si_ref[0, rows, :]
    w_rows = ikw_ref[rows, :].astype(F32).T

    group = min(KEY_GROUP, seq)
    per_group = group // QB
    for n in range(1, seq // group + 1):
        @pl.when(j // per_group + 1 == n)
        def _(n=n):
            _dsa_block(n * group, j, q_ref, (g0_ref, g1_ref), (iq0_ref, iq1_ref), w_rows,
                       ca, sa, ci, si, qn_ref, o_ref, k_s, vt_s, ik_s, key_s, val_s,
                       topk=topk, first_half=first_half, low_group=low_group)


def _dsa(proj, tables, qn_g, kn_g, batch, seq):
    nb = seq // Q_BLOCK
    topk = min(TOPK_MAX, seq // 4)
    ca, sa, ci, si = tables
    half = BRANCH_WIDTH // 2
    qrow = lambda width, col: pl.BlockSpec((Q_BLOCK, width), lambda b, j: (b * nb + j, col // width))
    krow = lambda col: pl.BlockSpec((seq, LANES), lambda b, j: (b, col // LANES))
    tab = pl.BlockSpec((1, seq, LANES), lambda b, j: (b, 0, 0))
    vec = pl.BlockSpec((1, LANES), lambda b, j: (0, 0))
    return pl.pallas_call(
        functools.partial(_dsa_kernel, seq=seq, topk=topk),
        grid=(batch, nb),
        in_specs=[qrow(BRANCH_WIDTH, COL_AT_Q),
                  qrow(half, COL_AT_G), qrow(half, COL_AT_G + half),
                  qrow(half, COL_IX_Q), qrow(half, COL_IX_Q + half),
                  krow(COL_AT_K), krow(COL_AT_V), krow(COL_IX_KW),
                  tab, tab, tab, tab, vec, vec],
        out_specs=pl.BlockSpec((Q_BLOCK, BRANCH_WIDTH), lambda b, j: (b * nb + j, 0)),
        out_shape=jax.ShapeDtypeStruct((batch * seq, BRANCH_WIDTH), BF16),
        scratch_shapes=[pltpu.VMEM((seq, LANES), BF16), pltpu.VMEM((LANES, seq), BF16),
                        pltpu.VMEM((seq, LANES), BF16),
                        pltpu.VMEM((seq, Q_BLOCK), F32), pltpu.VMEM((seq, Q_BLOCK), F32)],
        compiler_params=_cparams("parallel", "arbitrary"),
        name="dsa_mixer",
    )(proj, proj, proj, proj, proj, proj, proj, proj, ca, sa, ci, si, qn_g, kn_g)


def _s5prep_kernel(are_ref, aim_ref, ldt_ref, btr_ref, bti_ref, cr_ref, ci_ref, perm_ref,
                   t_ref, min_ref, nout_ref, al_ref):
    Lb, G16 = S5_BLOCK, S5_GROUP
    hp = lax.Precision.HIGHEST
    a_re, a_im = are_ref[0], aim_ref[0]
    dt = jnp.exp(ldt_ref[0])
    mag = jnp.exp(a_re * dt)
    ang = a_im * dt
    ab_r, ab_i = mag * jnp.cos(ang), mag * jnp.sin(ang)
    nr, ni = ab_r - 1.0, ab_i
    den = a_re * a_re + a_im * a_im
    fr = (nr * a_re + ni * a_im) / den
    fi = (ni * a_re - nr * a_im) / den
    bt_r, bt_i = btr_ref[0], bti_ref[0]
    bb_r = fr * bt_r - fi * bt_i
    bb_i = fr * bt_i + fi * bt_r
    c_r, c_i = cr_ref[0], ci_ref[0]

    pw_r, pw_i = [jnp.ones_like(ab_r)], [jnp.zeros_like(ab_r)]
    for _ in range(Lb):
        pr, pi = pw_r[-1], pw_i[-1]
        pw_r.append(pr * ab_r - pi * ab_i)
        pw_i.append(pr * ab_i + pi * ab_r)

    def readout(first):
        re = jnp.concatenate([c_r * pw_r[first + t] - c_i * pw_i[first + t] for t in range(Lb)], axis=0)
        im = jnp.concatenate([c_r * pw_i[first + t] + c_i * pw_r[first + t] for t in range(Lb)], axis=0)
        return re, im

    wc_r, wc_i = readout(0)
    taps = (lax.dot_general(bb_r, wc_r, _NT, precision=hp, preferred_element_type=F32)
            - lax.dot_general(bb_i, wc_i, _NT, precision=hp, preferred_element_type=F32))
    t_mat = jnp.concatenate(
        [taps] + [jnp.concatenate([jnp.zeros((G16, s * G16), F32), taps[:, :(Lb - s) * G16]], axis=1)
                  for s in range(1, Lb)], axis=0)
    m_r = jnp.concatenate([pw_r[Lb - 1 - s] * bb_r - pw_i[Lb - 1 - s] * bb_i for s in range(Lb)], axis=0)
    m_i = jnp.concatenate([pw_r[Lb - 1 - s] * bb_i + pw_i[Lb - 1 - s] * bb_r for s in range(Lb)], axis=0)
    m_in = jnp.concatenate([m_r, m_i, m_i, m_r], axis=1)
    n_r, n_i = readout(1)
    n_out = jnp.concatenate([n_r, -n_i], axis=1)

    perm = perm_ref[0]
    t_rows = jnp.dot(perm, t_mat.astype(BF16), preferred_element_type=F32).astype(BF16)
    t_ref[0] = lax.dot_general(t_rows, perm, _NT, preferred_element_type=F32).astype(BF16)
    min_ref[0] = jnp.dot(perm, m_in.astype(BF16), preferred_element_type=F32).astype(BF16)
    nout_ref[0] = jnp.dot(perm, n_out.astype(BF16), preferred_element_type=F32).astype(BF16)
    al_r, al_i = pw_r[Lb], pw_i[Lb]
    al_ref[0] = jnp.concatenate([jnp.concatenate([al_r, al_r], axis=1),
                                 jnp.concatenate([-al_i, al_i], axis=1),
                                 jnp.zeros((6, 2 * S5_STATE), F32)], axis=0)


def _s5_prepare(a_re, a_im, log_dt, b_re, b_im, c_re, c_im):
    G, P, K = S5_GROUPS, S5_STATE, S5_BLOCK * S5_GROUP
    half = LANES // S5_GROUP
    g3 = lambda shape: pl.BlockSpec((1,) + shape, lambda g: (g, 0, 0))
    return pl.pallas_call(
        _s5prep_kernel,
        grid=(G,),
        in_specs=[g3((1, P)), g3((1, P)), g3((1, 1)), g3((S5_GROUP, P)), g3((S5_GROUP, P)),
                  g3((S5_GROUP, P)), g3((S5_GROUP, P)),
                  pl.BlockSpec((1, K, K), lambda g: (g % half, 0, 0))],
        out_specs=[g3((K, K)), g3((K, 4 * P)), g3((K, 2 * P)), g3((8, 2 * P))],
        out_shape=[jax.ShapeDtypeStruct((G, K, K), BF16), jax.ShapeDtypeStruct((G, K, 4 * P), BF16),
                   jax.ShapeDtypeStruct((G, K, 2 * P), BF16), jax.ShapeDtypeStruct((G, 8, 2 * P), F32)],
        compiler_params=_cparams("parallel"),
        name="s5_prepare",
    )(a_re.reshape(G, 1, P), a_im.reshape(G, 1, P), log_dt.reshape(G, 1, 1),
      jnp.swapaxes(b_re, 1, 2), jnp.swapaxes(b_im, 1, 2), c_re, c_im, jnp.asarray(_s5_slot_perm(), BF16))


def _s5_slot(step, g):
    half = LANES // S5_GROUP
    return (step // half) * half + (g % half + step % half) % half


def _s5_slot_perm():
    half = LANES // S5_GROUP
    perm = np.zeros((half, S5_BLOCK * S5_GROUP, S5_BLOCK * S5_GROUP), np.float32)
    for g in range(half):
        for step in range(S5_BLOCK):
            for ch in range(S5_GROUP):
                perm[g, _s5_slot(step, g) * S5_GROUP + ch, step * S5_GROUP + ch] = 1.0
    return perm


def _s5_in_kernel(*refs, nblk):
    u_refs, o_ref = refs[:-1], refs[-1]
    half = LANES // S5_GROUP
    for k, u_ref in enumerate(u_refs):
        for step in range(S5_BLOCK):
            slab = u_ref[pl.ds(step, nblk, stride=S5_BLOCK), :]
            shift = (step % half) * S5_GROUP
            if shift:
                slab = pltpu.roll(slab, shift, 1)
            slab = slab.astype(BF16)
            for gi in range(half):
                g = k * half + gi
                lo = (_s5_slot(step, g) % half) * S5_GROUP
                base = (step // half) * LANES
                o_ref[g, :, base + lo:base + lo + S5_GROUP] = slab[:, lo:lo + S5_GROUP]


def _s5_out_kernel(*refs, nblk):
    y_ref, o_refs = refs[0], refs[1:]
    half = LANES // S5_GROUP
    slot_of_lane = lax.broadcasted_iota(jnp.int32, (1, LANES), 1) // S5_GROUP
    for k, o_ref in enumerate(o_refs):
        for step in range(S5_BLOCK):
            base = (step // half) * LANES
            slab = None
            for gi in range(half):
                g = k * half + gi
                src = y_ref[g, :, base:base + LANES]
                slab = src if slab is None else jnp.where(slot_of_lane == _s5_slot(step, g) % half, src, slab)
            slab = slab.astype(F32)
            shift = (step % half) * S5_GROUP
            if shift:
                slab = pltpu.roll(slab, LANES - shift, 1)
            o_ref[pl.ds(step, nblk, stride=S5_BLOCK), :] = slab


def _s5_kernel(u_ref, t_ref, min_ref, nout_ref, al_ref, y_ref, inj_s, injx_s, xp_s, *, batch, nblk):
    P2 = 2 * S5_STATE
    u = u_ref[0]
    inj = jnp.dot(u, min_ref[0], preferred_element_type=F32)
    inj_s[...] = inj[:, 0:P2]
    injx_s[...] = inj[:, P2:2 * P2]
    a_same = al_ref[0, 0:1, :]
    a_cross = al_ref[0, 1:2, :]

    def step(jb, carry):
        x, xs = carry
        rows = pl.ds(jb, batch, stride=nblk)
        xp_s[rows, :] = x
        return (x * a_same + xs * a_cross + inj_s[rows, :],
                xs * a_same - x * a_cross + injx_s[rows, :])

    z = jnp.zeros((batch, P2), F32)
    lax.fori_loop(0, nblk, step, (z, z), unroll=4)
    y = jnp.dot(u, t_ref[0], preferred_element_type=F32)
    y = y + lax.dot_general(xp_s[...].astype(BF16), nout_ref[0], _NT, preferred_element_type=F32)
    y_ref[0] = y.astype(BF16)


def _s5_scan(u32, ops, batch, seq):
    t_mat, m_in, n_out, al = ops
    G, P, K = S5_GROUPS, S5_STATE, S5_BLOCK * S5_GROUP
    nblk = seq // S5_BLOCK
    R = nblk * batch
    nslab = BRANCH_WIDTH // LANES
    slab = lambda k: pl.BlockSpec((seq, LANES), lambda b, k=k: (b, k))
    grouped = pl.BlockSpec((G, nblk, K), lambda b: (0, b, 0))
    u = pl.pallas_call(
        functools.partial(_s5_in_kernel, nblk=nblk),
        grid=(batch,),
        in_specs=[slab(k) for k in range(nslab)],
        out_specs=grouped,
        out_shape=jax.ShapeDtypeStruct((G, R, K), BF16),
        compiler_params=_cparams("parallel"),
        name="s5_relayout_in",
    )(*([u32] * nslab))
    g3 = lambda shape: pl.BlockSpec((1,) + shape, lambda g: (g, 0, 0))
    y = pl.pallas_call(
        functools.partial(_s5_kernel, batch=batch, nblk=nblk),
        grid=(G,),
        in_specs=[g3((R, K)), g3((K, K)), g3((K, 4 * P)), g3((K, 2 * P)), g3((8, 2 * P))],
        out_specs=g3((R, K)),
        out_shape=jax.ShapeDtypeStruct((G, R, K), BF16),
        scratch_shapes=[pltpu.VMEM((R, 2 * P), F32), pltpu.VMEM((R, 2 * P), F32),
                        pltpu.VMEM((R, 2 * P), F32)],
        compiler_params=_cparams("parallel"),
        name="s5_scan",
    )(u, t_mat, m_in, n_out, al)
    return pl.pallas_call(
        functools.partial(_s5_out_kernel, nblk=nblk),
        grid=(batch,),
        in_specs=[grouped],
        out_specs=[pl.BlockSpec((seq, LANES), lambda b: (b, 0)) for _ in range(nslab)],
        out_shape=[jax.ShapeDtypeStruct((batch * seq, LANES), F32) for _ in range(nslab)],
        compiler_params=_cparams("parallel"),
        name="s5_relayout_out",
    )(y)


def _merge_kernel(x_ref, gate_ref, ya_ref, yb_ref, y50_ref, y51_ref, y52_ref, y53_ref, u_ref, sg_ref,
                  m0_ref, m1_ref, m2_ref, dsk_ref, gw_ref, gb_ref, wb_ref, wo_ref, o_ref):
    y5 = jnp.concatenate([y50_ref[...], y51_ref[...], y52_ref[...], y53_ref[...]], axis=1)
    y = y5 + dsk_ref[...] * u_ref[...].astype(F32)
    y = jax.nn.gelu(y)
    glu = jnp.dot(y.astype(BF16), gw_ref[...], preferred_element_type=F32) + gb_ref[...]
    yc = (y * _sigmoid(glu) * _silu(sg_ref[...].astype(F32))).astype(BF16)
    merged = jnp.zeros(x_ref.shape, F32)
    for n, (yn, m_ref) in enumerate(((ya_ref[...], m0_ref), (yb_ref[...], m1_ref), (yc, m2_ref))):
        yd = jnp.dot(yn, wb_ref[n], preferred_element_type=F32)
        merged = merged + _sigmoid(m_ref[...].astype(F32)) * yd
    o_ref[...] = x_ref[...] + gate_ref[0] * jnp.dot(merged.astype(BF16), wo_ref[...],
                                                    preferred_element_type=F32)


def _merge(x2, gate, ya, yb, y5, proj, d_skip, glu_w, glu_b, w_branch, w_out, seq):
    M, D = x2.shape
    W = BRANCH_WIDTH
    tm = min(512, seq)
    per_batch = seq // tm
    row = lambda cols, colblk: pl.BlockSpec((tm, cols), lambda i: (i, colblk))
    full = lambda shape: pl.BlockSpec(shape, lambda i: (0,) * len(shape))
    return pl.pallas_call(
        _merge_kernel,
        grid=(M // tm,),
        in_specs=[row(D, 0),
                  pl.BlockSpec((1, 1, D), lambda i: (i // per_batch, 0, 0)),
                  row(W, 0), row(W, 0),
                  row(LANES, 0), row(LANES, 0), row(LANES, 0), row(LANES, 0),
                  row(W, COL_S5_U // W), row(W, COL_S5_G // W),
                  row(D, COL_MERGE // D), row(D, COL_MERGE // D + 1), row(D, COL_MERGE // D + 2),
                  full((1, W)), full((W, W)), full((1, W)), full((N_BRANCH, W, D)), full((D, D))],
        out_specs=row(D, 0),
        out_shape=jax.ShapeDtypeStruct((M, D), F32),
        compiler_params=_cparams("parallel"),
        name="merge_out",
    )(x2, gate, ya, yb, *y5, proj, proj, proj, proj, proj, d_skip, glu_w, glu_b, w_branch, w_out)


def kernel(x, c, positions, ada_w, ada_b, norm_g, w_in, hg_lb_logits, hg_onorm_g, at_qnorm_g,
           at_knorm_g, s5_a_re, s5_a_im, s5_log_dt, s5_b_re, s5_b_im, s5_c_re, s5_c_im, s5_d,
           s5_glu_w, s5_glu_b, w_branch, w_out):
    B, S, D = x.shape
    L = ada_w.shape[0]
    lb_all = _lower_bounds(hg_lb_logits.astype(F32))
    mod = _modulation(c, ada_w, ada_b)
    w_all = _permute_w_in(w_in)
    tables = _rope_tables(positions)
    x2 = x.reshape(B * S, D)
    for l in range(L):
        shift = mod[l, :, None, 0:D]
        scale = mod[l, :, None, D:2 * D]
        gate = mod[l, :, None, 2 * D:3 * D]
        proj, hgf, u32 = _projection(x2, norm_g[l][None, :], shift, scale, w_all, l, S)
        ya = _hgrn(proj, hgf, lb_all[l][None, :], hg_onorm_g[l][None, :], B, S)
        yb = _dsa(proj, tables, at_qnorm_g[l][None, :], at_knorm_g[l][None, :], B, S)
        ops = _s5_prepare(s5_a_re[l], s5_a_im[l], s5_log_dt[l], s5_b_re[l], s5_b_im[l],
                          s5_c_re[l], s5_c_im[l])
        y5 = _s5_scan(u32, ops, B, S)
        x2 = _merge(x2, gate, ya, yb, y5, proj, s5_d[l][None, :], s5_glu_w[l].astype(BF16),
                    s5_glu_b[l][None, :], w_branch[l].astype(BF16), w_out[l].astype(BF16), S)
    return x2.reshape(B, S, D)
```

```python
import functools
import math

import jax
import jax.numpy as jnp
import numpy as np
from jax import lax
from jax.experimental import pallas as pl
from jax.experimental.pallas import tpu as pltpu

F32 = jnp.float32
BF16 = jnp.bfloat16

D_MODEL = 1024
DEPTH = 4
BRANCH_WIDTH = 512
N_BRANCH = 3
EPS = 1e-6
NEG_BIG = -1e30
HG_HEADS = 4
HG_DIM = 128
AT_HEADS = 4
AT_DIM = 128
IDX_HEADS = 8
IDX_DIM = 64
TOPK_MAX = 256
Q_BLOCK = 128
ROPE_THETA = 10000.0
S5_GROUP = 16
S5_GROUPS = BRANCH_WIDTH // S5_GROUP
S5_STATE = 64
N_IN = 8008

LANES = 128
HG_CHUNK = 128
HG_SUB = 8
S5_BLOCK = 16
KEY_GROUP = 256
VMEM_LIMIT = 52 * 1024 * 1024

REF_ALIGNED = 3840
REF_TAIL = 3912
COL_S5_U = 0
COL_S5_G = 512
COL_MERGE = 1024
COL_HG_Q = 4096
COL_HG_F = 4608
COL_HG_I = 5120
COL_HG_G = 5632
COL_AT_Q = 6144
COL_AT_K = 6656
COL_AT_V = 6784
COL_AT_G = 6912
COL_IX_Q = 7424
COL_IX_KW = 7936
N_PROJ = 8192
PROJ_TN = 2048
assert COL_HG_F % PROJ_TN + BRANCH_WIDTH <= PROJ_TN and COL_S5_U % PROJ_TN + BRANCH_WIDTH <= PROJ_TN

_NT = (((1,), (1,)), ((), ()))
_TN = (((0,), (0,)), ((), ()))


def _cparams(*sem):
    return pltpu.CompilerParams(dimension_semantics=sem, vmem_limit_bytes=VMEM_LIMIT)


def _sigmoid(x):
    return 1.0 / (1.0 + jnp.exp(-x))


def _silu(x):
    return x * _sigmoid(x)


def _lb_kernel(z_ref, o_ref):
    z = z_ref[...]
    e = jnp.exp(z - jnp.max(z, axis=0, keepdims=True))
    p = e / jnp.sum(e, axis=0, keepdims=True)
    acc = jnp.zeros_like(p[0:1])
    for l in range(z.shape[0]):
        acc = acc + p[l:l + 1]
        o_ref[l:l + 1, :] = acc - p[0:1]


def _lower_bounds(logits):
    return pl.pallas_call(
        _lb_kernel, out_shape=jax.ShapeDtypeStruct(logits.shape, F32), name="hg_lower_bounds",
    )(logits)


def _mod_kernel(c_ref, w_ref, b_ref, o_ref):
    c = c_ref[...]
    o_ref[0] = jnp.dot(_silu(c), w_ref[0], precision=lax.Precision.HIGHEST,
                       preferred_element_type=F32) + b_ref[0]


def _modulation(c, ada_w, ada_b):
    L, D, N = ada_w.shape
    B = c.shape[0]
    tn = 512
    return pl.pallas_call(
        _mod_kernel,
        grid=(L, N // tn),
        in_specs=[pl.BlockSpec((B, D), lambda l, n: (0, 0)),
                  pl.BlockSpec((1, D, tn), lambda l, n: (l, 0, n)),
                  pl.BlockSpec((1, 1, tn), lambda l, n: (l, 0, n))],
        out_specs=pl.BlockSpec((1, B, tn), lambda l, n: (l, 0, n)),
        out_shape=jax.ShapeDtypeStruct((L, B, N), F32),
        compiler_params=_cparams("parallel", "parallel"),
        name="adaln_modulation",
    )(c, ada_w, ada_b.reshape(L, 1, N))


def _wprep_kernel(w_ref, o_ref):
    tail = N_IN - REF_TAIL
    o_ref[0, :, 0:tail] = w_ref[0, :, REF_TAIL:N_IN].astype(BF16)
    o_ref[0, :, tail:tail + REF_ALIGNED] = w_ref[0, :, 0:REF_ALIGNED].astype(BF16)
    kw = w_ref[0, :, REF_ALIGNED:REF_ALIGNED + LANES]
    lane = lax.broadcasted_iota(jnp.int32, kw.shape, 1)
    o_ref[0, :, COL_IX_KW:COL_IX_KW + LANES] = jnp.where(lane < REF_TAIL - REF_ALIGNED, kw, 0.0).astype(BF16)
    o_ref[0, :, COL_IX_KW + LANES:N_PROJ] = jnp.zeros((kw.shape[0], N_PROJ - COL_IX_KW - LANES), BF16)


def _permute_w_in(w_in):
    L, D, N = w_in.shape
    tr = 128
    return pl.pallas_call(
        _wprep_kernel,
        grid=(L, D // tr),
        in_specs=[pl.BlockSpec((1, tr, N), lambda l, r: (l, r, 0))],
        out_specs=pl.BlockSpec((1, tr, N_PROJ), lambda l, r: (l, r, 0)),
        out_shape=jax.ShapeDtypeStruct((L, D, N_PROJ), BF16),
        compiler_params=_cparams("parallel", "parallel"),
        name="w_in_prepare",
    )(w_in)


def _proj_kernel(x_ref, g_ref, shift_ref, scale_ref, w_ref, o_ref, f_ref, u_ref, h_ref):
    j = pl.program_id(1)

    @pl.when(j == 0)
    def _():
        x = x_ref[...]
        r = lax.rsqrt(jnp.mean(x * x, axis=-1, keepdims=True) + EPS)
        h = (x * r) * g_ref[...] * (1.0 + scale_ref[0]) + shift_ref[0]
        h_ref[...] = h.astype(BF16)

    acc = jnp.dot(h_ref[...], w_ref[0], preferred_element_type=F32)
    o_ref[...] = acc.astype(BF16)

    @pl.when(j == COL_HG_F // PROJ_TN)
    def _():
        off = COL_HG_F % PROJ_TN
        f_ref[...] = acc[:, off:off + BRANCH_WIDTH]

    @pl.when(j == COL_S5_U // PROJ_TN)
    def _():
        off = COL_S5_U % PROJ_TN
        u_ref[...] = acc[:, off:off + BRANCH_WIDTH]


def _projection(x2, norm_g, shift, scale, w_all, layer, seq):
    M, D = x2.shape
    tm = min(1024, seq)
    per_batch = seq // tm
    return pl.pallas_call(
        _proj_kernel,
        grid=(M // tm, N_PROJ // PROJ_TN),
        in_specs=[pl.BlockSpec((tm, D), lambda i, j: (i, 0)),
                  pl.BlockSpec((1, D), lambda i, j: (0, 0)),
                  pl.BlockSpec((1, 1, D), lambda i, j: (i // per_batch, 0, 0)),
                  pl.BlockSpec((1, 1, D), lambda i, j: (i // per_batch, 0, 0)),
                  pl.BlockSpec((1, D, PROJ_TN), lambda i, j: (layer, 0, j))],
        out_specs=[pl.BlockSpec((tm, PROJ_TN), lambda i, j: (i, j)),
                   pl.BlockSpec((tm, BRANCH_WIDTH), lambda i, j: (i, 0)),
                   pl.BlockSpec((tm, BRANCH_WIDTH), lambda i, j: (i, 0))],
        out_shape=[jax.ShapeDtypeStruct((M, N_PROJ), BF16),
                   jax.ShapeDtypeStruct((M, BRANCH_WIDTH), F32),
                   jax.ShapeDtypeStruct((M, BRANCH_WIDTH), F32)],
        scratch_shapes=[pltpu.VMEM((tm, D), BF16)],
        compiler_params=_cparams("parallel", "arbitrary"),
        name="norm_in_proj",
    )(x2, norm_g, shift, scale, w_all)


def _split3(x):
    hi = x.astype(BF16)
    r1 = x - hi.astype(F32)
    mid = r1.astype(BF16)
    lo = (r1 - mid.astype(F32)).astype(BF16)
    return hi, mid, lo


def _hgrn_kernel(q_ref, f_ref, i_ref, g_ref, lb_ref, on_ref, o_ref, state_ref, *, rows):
    C, c = HG_CHUNK, HG_SUB
    nsub = C // c

    @pl.when(pl.program_id(2) == 0)
    def _():
        state_ref[...] = jnp.zeros_like(state_ref)

    lb = lb_ref[...]
    row_id = lax.broadcasted_iota(jnp.int32, (C, C), 0)
    col_id = lax.broadcasted_iota(jnp.int32, (C, C), 1)
    tril = jnp.where(col_id <= row_id, 1.0, 0.0).astype(BF16)
    sub_row = lax.broadcasted_iota(jnp.int32, (c, C), 0)
    sub_col = lax.broadcasted_iota(jnp.int32, (c, C), 1)

    for ci in range(rows // C):
        sl = slice(ci * C, (ci + 1) * C)
        q = q_ref[sl, :].astype(F32)
        v16 = i_ref[sl, :]
        fg = lb + (1.0 - lb) * _sigmoid(f_ref[sl, :])
        lf = jnp.log2(jnp.maximum(fg, 1e-30))
        k = 1.0 - fg
        hi, mid, lo = _split3(lf)
        b = (jnp.dot(tril, hi, preferred_element_type=F32)
             + jnp.dot(tril, mid, preferred_element_type=F32)
             + jnp.dot(tril, lo, preferred_element_type=F32))
        b_last = b[C - 1:C, :]

        state_t = state_ref[...]
        q0 = (q * jnp.exp2(b)).astype(BF16)
        o = lax.dot_general(q0, state_t.astype(BF16), _NT, preferred_element_type=F32)

        row_blocks = []
        k_run = []
        for t in range(nsub):
            r0 = t * c
            bt = b[r0:r0 + c, :]
            qt = q[r0:r0 + c, :]
            diag = jnp.zeros((c, C), F32)
            for s in range(c):
                e = jnp.exp2(bt - b[r0 + s:r0 + s + 1, :])
                col = jnp.sum(qt * e * k[r0 + s:r0 + s + 1, :], axis=-1, keepdims=True)
                diag = jnp.where(sub_col == r0 + s, col, diag)
            diag = jnp.where(sub_col - r0 <= sub_row, diag, 0.0)
            if t == 0:
                row_blocks.append(diag)
                continue
            r_t = b[r0 - 1:r0, :]
            if k_run:
                step = jnp.exp2(r_t - b[r0 - c - 1:r0 - c, :])
                k_run = [kb * step for kb in k_run]
            k_run.append(k[r0 - c:r0, :] * jnp.exp2(r_t - b[r0 - c:r0, :]))
            k_t = jnp.concatenate(k_run + [jnp.zeros((C - r0, C), F32)], axis=0).astype(BF16)
            q_t = (qt * jnp.exp2(bt - r_t)).astype(BF16)
            off = lax.dot_general(q_t, k_t, _NT, preferred_element_type=F32)
            row_blocks.append(off + diag)
        scores = jnp.concatenate(row_blocks, axis=0)
        o = o + jnp.dot(scores.astype(BF16), v16, preferred_element_type=F32)

        k_dec = (k * jnp.exp2(b_last - b)).astype(BF16)
        state_ref[...] = (state_t * jnp.exp2(b_last)
                          + lax.dot_general(v16, k_dec, _TN, preferred_element_type=F32))

        r = lax.rsqrt(jnp.mean(o * o, axis=-1, keepdims=True) + EPS)
        o_ref[sl, :] = ((o * r) * on_ref[...] * _silu(g_ref[sl, :].astype(F32))).astype(BF16)


def _hgrn(proj, hgf, lb, onorm_g, batch, seq):
    rows = min(1024, seq)
    nr = seq // rows
    cb = lambda base: (lambda b, h, r: (b * nr + r, base // HG_DIM + h))
    return pl.pallas_call(
        functools.partial(_hgrn_kernel, rows=rows),
        grid=(batch, HG_HEADS, nr),
        in_specs=[pl.BlockSpec((rows, HG_DIM), cb(COL_HG_Q)),
                  pl.BlockSpec((rows, HG_DIM), cb(0)),
                  pl.BlockSpec((rows, HG_DIM), cb(COL_HG_I)),
                  pl.BlockSpec((rows, HG_DIM), cb(COL_HG_G)),
                  pl.BlockSpec((1, HG_DIM), lambda b, h, r: (0, h)),
                  pl.BlockSpec((1, HG_DIM), lambda b, h, r: (0, 0))],
        out_specs=pl.BlockSpec((rows, HG_DIM), lambda b, h, r: (b * nr + r, h)),
        out_shape=jax.ShapeDtypeStruct((batch * seq, BRANCH_WIDTH), BF16),
        scratch_shapes=[pltpu.VMEM((HG_DIM, HG_DIM), F32)],
        compiler_params=_cparams("parallel", "parallel", "arbitrary"),
        name="hgrn2_mixer",
    )(proj, hgf, proj, proj, lb, onorm_g)


def _rope_tables(positions):
    pos = positions.astype(F32)[..., None]

    def tables(dim, reps):
        inv = ROPE_THETA ** (-jnp.arange(0, dim, 2, dtype=F32) / dim)
        ang = pos * inv
        c, s = jnp.cos(ang), jnp.sin(ang)
        return (jnp.concatenate([c, c] * reps, axis=-1),
                jnp.concatenate([-s, s] * reps, axis=-1))

    return tables(AT_DIM, 1) + tables(IDX_DIM, LANES // IDX_DIM)


def _rope_full(x, cos, sin_signed):
    return x * cos + pltpu.roll(x, AT_DIM // 2, 1) * sin_signed


def _rope_idx(x, cos, sin_signed, first_half):
    h = IDX_DIM // 2
    partner = jnp.where(first_half, pltpu.roll(x, LANES - h, 1), pltpu.roll(x, h, 1))
    return x * cos + partner * sin_signed


def _col_reduce(x, op):
    part = 64
    if x.shape[0] > part:
        x = op(x.reshape(x.shape[0] // part, part, x.shape[1]), axis=0)
    return op(x, axis=0, keepdims=True)


def _dsa_block(width, m, q_ref, g_refs, iq_refs, ikw_ref, ca_ref, sa_ref, ci_ref, si_ref, qn_ref, o_ref,
               k_s, vt_s, ik_s, key_s, val_s, *, topk, nsub, first_half, low_group):
    QB = Q_BLOCK
    kk_s, ikk_s = k_s.at[0:width, :], ik_s.at[0:width, :]
    val_w = val_s.at[0:width, :]
    kpos = lax.broadcasted_iota(jnp.int32, (width, 1), 0)
    per_slab = LANES // IDX_DIM

    def table_rows(sub):
        return pl.ds(pl.multiple_of((m * nsub + sub) * QB, QB), QB)

    def causal_mask(sub):
        qpos = (m * nsub + sub) * QB + lax.broadcasted_iota(jnp.int32, (1, QB), 1)
        return kpos <= qpos

    for sub in range(nsub):
        qrows = slice(sub * QB, (sub + 1) * QB)
        ci, si = ci_ref[0, table_rows(sub), :], si_ref[0, table_rows(sub), :]
        w_rows = ikw_ref[table_rows(sub), :].astype(F32).T
        isc = jnp.zeros((width, QB), F32)
        for slab in range(IDX_HEADS // per_slab):
            iq_ref = iq_refs[slab // 2]
            off = (slab % 2) * LANES
            xr = _rope_idx(iq_ref[qrows, off:off + LANES].astype(F32), ci, si, first_half)
            for par in range(per_slab):
                h = slab * per_slab + par
                xm = jnp.where(low_group if par == 0 else jnp.logical_not(low_group), xr, 0.0)
                rel = lax.dot_general(ikk_s[...], xm.astype(BF16), _NT, preferred_element_type=F32)
                isc = isc + jnp.maximum(rel, 0.0) * w_rows[IDX_DIM + h:IDX_DIM + h + 1, :]
        key_s[sub, 0:width, :] = jnp.where(causal_mask(sub), isc, NEG_BIG)

    kf = float(topk)
    thr = None
    if width > topk:
        def as_score(image):
            bits = jnp.where(image < 0, image ^ jnp.int32(0x7FFFFFFF), image)
            return pltpu.bitcast(bits, F32)

        def count_ge(sub, cand):
            return _col_reduce(jnp.where(key_s[sub, 0:width, :] >= as_score(cand), 1.0, 0.0), jnp.sum)

        zero = jnp.zeros((1, QB), jnp.int32)
        lowest = jnp.full((1, QB), -2**31, jnp.int32)
        init = tuple(jnp.where(count_ge(sub, zero) >= kf, zero, lowest) for sub in range(nsub))

        def bisect(i, ans):
            bit = jnp.int32(1) << (30 - i)
            return tuple(jnp.where(count_ge(sub, a + bit) >= kf, a + bit, a) for sub, a in enumerate(ans))

        thr = [as_score(a) for a in lax.fori_loop(0, 31, bisect, init)]

    scale = math.log2(math.e) / math.sqrt(AT_DIM)
    for sub in range(nsub):
        qrows = slice(sub * QB, (sub + 1) * QB)
        causal = causal_mask(sub)
        if thr is None:
            val_w[...] = jnp.where(causal, 1.0, 0.0)
        else:
            key = key_s[sub, 0:width, :]
            gt = key > thr[sub]
            eq = key == thr[sub]
            need = kf - _col_reduce(jnp.where(gt, 1.0, 0.0), jnp.sum)
            rr = lax.broadcasted_iota(jnp.int32, (LANES, LANES), 0)
            cc = lax.broadcasted_iota(jnp.int32, (LANES, LANES), 1)
            lower = jnp.where(cc <= rr, 1.0, 0.0).astype(BF16)
            seen = jnp.zeros((1, QB), F32)
            for cidx in range(width // LANES):
                sl = slice(cidx * LANES, (cidx + 1) * LANES)
                eq_c = jnp.where(eq[sl, :], 1.0, 0.0)
                rank = jnp.dot(lower, eq_c.astype(BF16), preferred_element_type=F32) + seen
                take = jnp.where(gt[sl, :], 1.0, jnp.where(rank <= need, eq_c, 0.0))
                val_w[sl, :] = jnp.where(causal[sl, :], take, 0.0)
                seen = rank[LANES - 1:LANES, :]

        ca, sa = ca_ref[0, table_rows(sub), :], sa_ref[0, table_rows(sub), :]
        valid = val_w[...] > 0.0
        for h in range(AT_HEADS):
            hs = slice(h * AT_DIM, (h + 1) * AT_DIM)
            qh = q_ref[qrows, hs].astype(F32)
            r = lax.rsqrt(jnp.mean(qh * qh, axis=-1, keepdims=True) + EPS)
            qh = _rope_full((qh * r) * qn_ref[...], ca, sa) * scale
            logits = lax.dot_general(kk_s[...], qh.astype(BF16), _NT, preferred_element_type=F32)
            logits = jnp.where(valid, logits, NEG_BIG)
            p = jnp.exp2(logits - _col_reduce(logits, jnp.max))
            denom = _col_reduce(p, jnp.sum)
            oh_t = jnp.dot(vt_s[:, 0:width], p.astype(BF16), preferred_element_type=F32) / denom
            g_ref = g_refs[h // 2]
            goff = (h % 2) * AT_DIM
            o_ref[qrows, hs] = (oh_t.T * _silu(g_ref[qrows, goff:goff + AT_DIM].astype(F32))).astype(BF16)


def _dsa_kernel(q_ref, g0_ref, g1_ref, iq0_ref, iq1_ref, k_ref, v_ref, ikw_ref,
                ca_ref, sa_ref, ci_ref, si_ref, qn_ref, kn_ref, o_ref,
                k_s, vt_s, ik_s, key_s, val_s, *, seq, topk, nsub):
    m = pl.program_id(1)
    lane = lax.broadcasted_iota(jnp.int32, (1, LANES), 1)
    first_half = (lane % IDX_DIM) < (IDX_DIM // 2)
    low_group = lane < IDX_DIM

    @pl.when(m == 0)
    def _():
        kk = k_ref[...].astype(F32)
        r = lax.rsqrt(jnp.mean(kk * kk, axis=-1, keepdims=True) + EPS)
        kk = (kk * r) * kn_ref[...]
        k_s[...] = _rope_full(kk, ca_ref[0], sa_ref[0]).astype(BF16)
        vt_s[...] = v_ref[...].astype(F32).T.astype(BF16)
        ik = _rope_idx(ikw_ref[...].astype(F32), ci_ref[0], si_ref[0], first_half)
        ik_s[...] = jnp.where(low_group, ik, pltpu.roll(ik, IDX_DIM, 1)).astype(BF16)

    group = nsub * Q_BLOCK
    for n in range(1, seq // group + 1):
        @pl.when(m + 1 == n)
        def _(n=n):
            _dsa_block(n * group, m, q_ref, (g0_ref, g1_ref), (iq0_ref, iq1_ref), ikw_ref,
                       ca_ref, sa_ref, ci_ref, si_ref, qn_ref, o_ref, k_s, vt_s, ik_s, key_s, val_s,
                       topk=topk, nsub=nsub, first_half=first_half, low_group=low_group)


def _dsa(proj, tables, qn_g, kn_g, batch, seq):
    group = min(KEY_GROUP, seq)
    nsub = group // Q_BLOCK
    ng = seq // group
    topk = min(TOPK_MAX, seq // 4)
    ca, sa, ci, si = tables
    half = BRANCH_WIDTH // 2
    qrow = lambda width, col: pl.BlockSpec((group, width), lambda b, m: (b * ng + m, col // width))
    krow = lambda col: pl.BlockSpec((seq, LANES), lambda b, m: (b, col // LANES))
    tab = pl.BlockSpec((1, seq, LANES), lambda b, m: (b, 0, 0))
    vec = pl.BlockSpec((1, LANES), lambda b, m: (0, 0))
    return pl.pallas_call(
        functools.partial(_dsa_kernel, seq=seq, topk=topk, nsub=nsub),
        grid=(batch, ng),
        in_specs=[qrow(BRANCH_WIDTH, COL_AT_Q),
                  qrow(half, COL_AT_G), qrow(half, COL_AT_G + half),
                  qrow(half, COL_IX_Q), qrow(half, COL_IX_Q + half),
                  krow(COL_AT_K), krow(COL_AT_V), krow(COL_IX_KW),
                  tab, tab, tab, tab, vec, vec],
        out_specs=pl.BlockSpec((group, BRANCH_WIDTH), lambda b, m: (b * ng + m, 0)),
        out_shape=jax.ShapeDtypeStruct((batch * seq, BRANCH_WIDTH), BF16),
        scratch_shapes=[pltpu.VMEM((seq, LANES), BF16), pltpu.VMEM((LANES, seq), BF16),
                        pltpu.VMEM((seq, LANES), BF16),
                        pltpu.VMEM((nsub, seq, Q_BLOCK), F32), pltpu.VMEM((seq, Q_BLOCK), F32)],
        compiler_params=_cparams("parallel", "arbitrary"),
        name="dsa_mixer",
    )(proj, proj, proj, proj, proj, proj, proj, proj, ca, sa, ci, si, qn_g, kn_g)


def _s5prep_kernel(are_ref, aim_ref, ldt_ref, btr_ref, bti_ref, cr_ref, ci_ref, perm_ref,
                   t_ref, min_ref, nout_ref, al_ref):
    Lb, G16 = S5_BLOCK, S5_GROUP
    hp = lax.Precision.HIGHEST
    a_re, a_im = are_ref[0], aim_ref[0]
    dt = jnp.exp(ldt_ref[0])
    mag = jnp.exp(a_re * dt)
    ang = a_im * dt
    ab_r, ab_i = mag * jnp.cos(ang), mag * jnp.sin(ang)
    nr, ni = ab_r - 1.0, ab_i
    den = a_re * a_re + a_im * a_im
    fr = (nr * a_re + ni * a_im) / den
    fi = (ni * a_re - nr * a_im) / den
    bt_r, bt_i = btr_ref[0], bti_ref[0]
    bb_r = fr * bt_r - fi * bt_i
    bb_i = fr * bt_i + fi * bt_r
    c_r, c_i = cr_ref[0], ci_ref[0]

    pw_r, pw_i = [jnp.ones_like(ab_r)], [jnp.zeros_like(ab_r)]
    for _ in range(Lb):
        pr, pi = pw_r[-1], pw_i[-1]
        pw_r.append(pr * ab_r - pi * ab_i)
        pw_i.append(pr * ab_i + pi * ab_r)

    def readout(first):
        re = jnp.concatenate([c_r * pw_r[first + t] - c_i * pw_i[first + t] for t in range(Lb)], axis=0)
        im = jnp.concatenate([c_r * pw_i[first + t] + c_i * pw_r[first + t] for t in range(Lb)], axis=0)
        return re, im

    wc_r, wc_i = readout(0)
    taps = (lax.dot_general(bb_r, wc_r, _NT, precision=hp, preferred_element_type=F32)
            - lax.dot_general(bb_i, wc_i, _NT, precision=hp, preferred_element_type=F32))
    t_mat = jnp.concatenate(
        [taps] + [jnp.concatenate([jnp.zeros((G16, s * G16), F32), taps[:, :(Lb - s) * G16]], axis=1)
                  for s in range(1, Lb)], axis=0)
    m_r = jnp.concatenate([pw_r[Lb - 1 - s] * bb_r - pw_i[Lb - 1 - s] * bb_i for s in range(Lb)], axis=0)
    m_i = jnp.concatenate([pw_r[Lb - 1 - s] * bb_i + pw_i[Lb - 1 - s] * bb_r for s in range(Lb)], axis=0)
    m_in = jnp.concatenate([m_r, m_i, m_i, m_r], axis=1)
    n_r, n_i = readout(1)
    n_out = jnp.concatenate([n_r, -n_i], axis=1)

    perm = perm_ref[0]
    t_rows = jnp.dot(perm, t_mat.astype(BF16), preferred_element_type=F32).astype(BF16)
    t_ref[0] = lax.dot_general(t_rows, perm, _NT, preferred_element_type=F32).astype(BF16)
    min_ref[0] = jnp.dot(perm, m_in.astype(BF16), preferred_element_type=F32).astype(BF16)
    nout_ref[0] = jnp.dot(perm, n_out.astype(BF16), preferred_element_type=F32).astype(BF16)
    al_r, al_i = pw_r[Lb], pw_i[Lb]
    al_ref[0] = jnp.concatenate([jnp.concatenate([al_r, al_r], axis=1),
                                 jnp.concatenate([-al_i, al_i], axis=1),
                                 jnp.zeros((6, 2 * S5_STATE), F32)], axis=0)


def _s5_prepare(a_re, a_im, log_dt, b_re, b_im, c_re, c_im):
    G, P, K = S5_GROUPS, S5_STATE, S5_BLOCK * S5_GROUP
    half = LANES // S5_GROUP
    g3 = lambda shape: pl.BlockSpec((1,) + shape, lambda g: (g, 0, 0))
    return pl.pallas_call(
        _s5prep_kernel,
        grid=(G,),
        in_specs=[g3((1, P)), g3((1, P)), g3((1, 1)), g3((S5_GROUP, P)), g3((S5_GROUP, P)),
                  g3((S5_GROUP, P)), g3((S5_GROUP, P)),
                  pl.BlockSpec((1, K, K), lambda g: (g % half, 0, 0))],
        out_specs=[g3((K, K)), g3((K, 4 * P)), g3((K, 2 * P)), g3((8, 2 * P))],
        out_shape=[jax.ShapeDtypeStruct((G, K, K), BF16), jax.ShapeDtypeStruct((G, K, 4 * P), BF16),
                   jax.ShapeDtypeStruct((G, K, 2 * P), BF16), jax.ShapeDtypeStruct((G, 8, 2 * P), F32)],
        compiler_params=_cparams("parallel"),
        name="s5_prepare",
    )(a_re.reshape(G, 1, P), a_im.reshape(G, 1, P), log_dt.reshape(G, 1, 1),
      jnp.swapaxes(b_re, 1, 2), jnp.swapaxes(b_im, 1, 2), c_re, c_im, jnp.asarray(_s5_slot_perm(), BF16))


def _s5_slot(step, g):
    half = LANES // S5_GROUP
    return (step // half) * half + (g % half + step % half) % half


def _s5_slot_perm():
    half = LANES // S5_GROUP
    perm = np.zeros((half, S5_BLOCK * S5_GROUP, S5_BLOCK * S5_GROUP), np.float32)
    for g in range(half):
        for step in range(S5_BLOCK):
            for ch in range(S5_GROUP):
                perm[g, _s5_slot(step, g) * S5_GROUP + ch, step * S5_GROUP + ch] = 1.0
    return perm


def _s5_in_kernel(*refs, nblk):
    u_refs, o_ref = refs[:-1], refs[-1]
    half = LANES // S5_GROUP
    for k, u_ref in enumerate(u_refs):
        for step in range(S5_BLOCK):
            slab = u_ref[pl.ds(step, nblk, stride=S5_BLOCK), :]
            shift = (step % half) * S5_GROUP
            if shift:
                slab = pltpu.roll(slab, shift, 1)
            slab = slab.astype(BF16)
            for gi in range(half):
                g = k * half + gi
                lo = (_s5_slot(step, g) % half) * S5_GROUP
                base = (step // half) * LANES
                o_ref[g, :, base + lo:base + lo + S5_GROUP] = slab[:, lo:lo + S5_GROUP]


def _s5_out_kernel(*refs, nblk):
    y_ref, o_refs = refs[0], refs[1:]
    half = LANES // S5_GROUP
    slot_of_lane = lax.broadcasted_iota(jnp.int32, (1, LANES), 1) // S5_GROUP
    for k, o_ref in enumerate(o_refs):
        for step in range(S5_BLOCK):
            base = (step // half) * LANES
            slab = None
            for gi in range(half):
                g = k * half + gi
                src = y_ref[g, :, base:base + LANES]
                slab = src if slab is None else jnp.where(slot_of_lane == _s5_slot(step, g) % half, src, slab)
            slab = slab.astype(F32)
            shift = (step % half) * S5_GROUP
            if shift:
                slab = pltpu.roll(slab, LANES - shift, 1)
            o_ref[pl.ds(step, nblk, stride=S5_BLOCK), :] = slab


def _s5_kernel(u_ref, t_ref, min_ref, nout_ref, al_ref, y_ref, inj_s, injx_s, xp_s, *, batch, nblk):
    P2 = 2 * S5_STATE
    u = u_ref[0]
    inj = jnp.dot(u, min_ref[0], preferred_element_type=F32)
    inj_s[...] = inj[:, 0:P2]
    injx_s[...] = inj[:, P2:2 * P2]
    a_same = al_ref[0, 0:1, :]
    a_cross = al_ref[0, 1:2, :]

    def step(jb, carry):
        x, xs = carry
        rows = pl.ds(jb, batch, stride=nblk)
        xp_s[rows, :] = x
        return (x * a_same + xs * a_cross + inj_s[rows, :],
                xs * a_same - x * a_cross + injx_s[rows, :])

    z = jnp.zeros((batch, P2), F32)
    lax.fori_loop(0, nblk, step, (z, z), unroll=4)
    y = jnp.dot(u, t_ref[0], preferred_element_type=F32)
    y = y + lax.dot_general(xp_s[...].astype(BF16), nout_ref[0], _NT, preferred_element_type=F32)
    y_ref[0] = y.astype(BF16)


def _s5_scan(u32, ops, batch, seq):
    t_mat, m_in, n_out, al = ops
    G, P, K = S5_GROUPS, S5_STATE, S5_BLOCK * S5_GROUP
    nblk = seq // S5_BLOCK
    R = nblk * batch
    nslab = BRANCH_WIDTH // LANES
    slab = lambda k: pl.BlockSpec((seq, LANES), lambda b, k=k: (b, k))
    grouped = pl.BlockSpec((G, nblk, K), lambda b: (0, b, 0))
    u = pl.pallas_call(
        functools.partial(_s5_in_kernel, nblk=nblk),
        grid=(batch,),
        in_specs=[slab(k) for k in range(nslab)],
        out_specs=grouped,
        out_shape=jax.ShapeDtypeStruct((G, R, K), BF16),
        compiler_params=_cparams("parallel"),
        name="s5_relayout_in",
    )(*([u32] * nslab))
    g3 = lambda shape: pl.BlockSpec((1,) + shape, lambda g: (g, 0, 0))
    y = pl.pallas_call(
        functools.partial(_s5_kernel, batch=batch, nblk=nblk),
        grid=(G,),
        in_specs=[g3((R, K)), g3((K, K)), g3((K, 4 * P)), g3((K, 2 * P)), g3((8, 2 * P))],
        out_specs=g3((R, K)),
        out_shape=jax.ShapeDtypeStruct((G, R, K), BF16),
        scratch_shapes=[pltpu.VMEM((R, 2 * P), F32), pltpu.VMEM((R, 2 * P), F32),
                        pltpu.VMEM((R, 2 * P), F32)],
        compiler_params=_cparams("parallel"),
        name="s5_scan",
    )(u, t_mat, m_in, n_out, al)
    return pl.pallas_call(
        functools.partial(_s5_out_kernel, nblk=nblk),
        grid=(batch,),
        in_specs=[grouped],
        out_specs=[pl.BlockSpec((seq, LANES), lambda b: (b, 0)) for _ in range(nslab)],
        out_shape=[jax.ShapeDtypeStruct((batch * seq, LANES), F32) for _ in range(nslab)],
        compiler_params=_cparams("parallel"),
        name="s5_relayout_out",
    )(y)


def _merge_kernel(x_ref, gate_ref, ya_ref, yb_ref, y50_ref, y51_ref, y52_ref, y53_ref, u_ref, sg_ref,
                  m0_ref, m1_ref, m2_ref, dsk_ref, gw_ref, gb_ref, wb_ref, wo_ref, o_ref):
    y5 = jnp.concatenate([y50_ref[...], y51_ref[...], y52_ref[...], y53_ref[...]], axis=1)
    y = y5 + dsk_ref[...] * u_ref[...].astype(F32)
    y = jax.nn.gelu(y)
    glu = jnp.dot(y.astype(BF16), gw_ref[...], preferred_element_type=F32) + gb_ref[...]
    yc = (y * _sigmoid(glu) * _silu(sg_ref[...].astype(F32))).astype(BF16)
    merged = jnp.zeros(x_ref.shape, F32)
    for n, (yn, m_ref) in enumerate(((ya_ref[...], m0_ref), (yb_ref[...], m1_ref), (yc, m2_ref))):
        yd = jnp.dot(yn, wb_ref[n], preferred_element_type=F32)
        merged = merged + _sigmoid(m_ref[...].astype(F32)) * yd
    o_ref[...] = x_ref[...] + gate_ref[0] * jnp.dot(merged.astype(BF16), wo_ref[...],
                                                    preferred_element_type=F32)


def _merge(x2, gate, ya, yb, y5, proj, d_skip, glu_w, glu_b, w_branch, w_out, seq):
    M, D = x2.shape
    W = BRANCH_WIDTH
    tm = min(512, seq)
    per_batch = seq // tm
    row = lambda cols, colblk: pl.BlockSpec((tm, cols), lambda i: (i, colblk))
    full = lambda shape: pl.BlockSpec(shape, lambda i: (0,) * len(shape))
    return pl.pallas_call(
        _merge_kernel,
        grid=(M // tm,),
        in_specs=[row(D, 0),
                  pl.BlockSpec((1, 1, D), lambda i: (i // per_batch, 0, 0)),
                  row(W, 0), row(W, 0),
                  row(LANES, 0), row(LANES, 0), row(LANES, 0), row(LANES, 0),
                  row(W, COL_S5_U // W), row(W, COL_S5_G // W),
                  row(D, COL_MERGE // D), row(D, COL_MERGE // D + 1), row(D, COL_MERGE // D + 2),
                  full((1, W)), full((W, W)), full((1, W)), full((N_BRANCH, W, D)), full((D, D))],
        out_specs=row(D, 0),
        out_shape=jax.ShapeDtypeStruct((M, D), F32),
        compiler_params=_cparams("parallel"),
        name="merge_out",
    )(x2, gate, ya, yb, *y5, proj, proj, proj, proj, proj, d_skip, glu_w, glu_b, w_branch, w_out)


def kernel(x, c, positions, ada_w, ada_b, norm_g, w_in, hg_lb_logits, hg_onorm_g, at_qnorm_g,
           at_knorm_g, s5_a_re, s5_a_im, s5_log_dt, s5_b_re, s5_b_im, s5_c_re, s5_c_im, s5_d,
           s5_glu_w, s5_glu_b, w_branch, w_out):
    B, S, D = x.shape
    L = ada_w.shape[0]
    lb_all = _lower_bounds(hg_lb_logits.astype(F32))
    mod = _modulation(c, ada_w, ada_b)
    w_all = _permute_w_in(w_in)
    tables = _rope_tables(positions)
    x2 = x.reshape(B * S, D)
    for l in range(L):
        shift = mod[l, :, None, 0:D]
        scale = mod[l, :, None, D:2 * D]
        gate = mod[l, :, None, 2 * D:3 * D]
        proj, hgf, u32 = _projection(x2, norm_g[l][None, :], shift, scale, w_all, l, S)
        ya = _hgrn(proj, hgf, lb_all[l][None, :], hg_onorm_g[l][None, :], B, S)
        yb = _dsa(proj, tables, at_qnorm_g[l][None, :], at_knorm_g[l][None, :], B, S)
        ops = _s5_prepare(s5_a_re[l], s5_a_im[l], s5_log_dt[l], s5_b_re[l], s5_b_im[l],
                          s5_c_re[l], s5_c_im[l])
        y5 = _s5_scan(u32, ops, B, S)
        x2 = _merge(x2, gate, ya, yb, y5, proj, s5_d[l][None, :], s5_glu_w[l].astype(BF16),
                    s5_glu_b[l][None, :], w_branch[l].astype(BF16), w_out[l].astype(BF16), S)
    return x2.reshape(B, S, D)
```

```python
import functools
import math

import jax
import jax.numpy as jnp
import numpy as np
from jax import lax
from jax.experimental import pallas as pl
from jax.experimental.pallas import tpu as pltpu

F32 = jnp.float32
BF16 = jnp.bfloat16

D_MODEL = 1024
DEPTH = 4
BRANCH_WIDTH = 512
N_BRANCH = 3
EPS = 1e-6
NEG_BIG = -1e30
HG_HEADS = 4
HG_DIM = 128
AT_HEADS = 4
AT_DIM = 128
IDX_HEADS = 8
IDX_DIM = 64
TOPK_MAX = 256
Q_BLOCK = 128
ROPE_THETA = 10000.0
S5_GROUP = 16
S5_GROUPS = BRANCH_WIDTH // S5_GROUP
S5_STATE = 64
N_IN = 8008

LANES = 128
HG_CHUNK = 128
HG_SUB = 8
S5_BLOCK = 16
KEY_GROUP = 512
VMEM_LIMIT = 52 * 1024 * 1024

REF_ALIGNED = 3840
REF_TAIL = 3912
COL_S5_U = 0
COL_S5_G = 512
COL_MERGE = 1024
COL_HG_Q = 4096
COL_HG_F = 4608
COL_HG_I = 5120
COL_HG_G = 5632
COL_AT_Q = 6144
COL_AT_K = 6656
COL_AT_V = 6784
COL_AT_G = 6912
COL_IX_Q = 7424
COL_IX_KW = 7936
N_PROJ = 8192
PROJ_TN = 2048
assert COL_HG_F % PROJ_TN + BRANCH_WIDTH <= PROJ_TN and COL_S5_U % PROJ_TN + BRANCH_WIDTH <= PROJ_TN

_NT = (((1,), (1,)), ((), ()))
_TN = (((0,), (0,)), ((), ()))


def _cparams(*sem):
    return pltpu.CompilerParams(dimension_semantics=sem, vmem_limit_bytes=VMEM_LIMIT)


def _sigmoid(x):
    return 1.0 / (1.0 + jnp.exp(-x))


def _silu(x):
    return x * _sigmoid(x)


def _lb_kernel(z_ref, o_ref):
    z = z_ref[...]
    e = jnp.exp(z - jnp.max(z, axis=0, keepdims=True))
    p = e / jnp.sum(e, axis=0, keepdims=True)
    acc = jnp.zeros_like(p[0:1])
    for l in range(z.shape[0]):
        acc = acc + p[l:l + 1]
        o_ref[l:l + 1, :] = acc - p[0:1]


def _lower_bounds(logits):
    return pl.pallas_call(
        _lb_kernel, out_shape=jax.ShapeDtypeStruct(logits.shape, F32), name="hg_lower_bounds",
    )(logits)


def _mod_kernel(c_ref, w_ref, b_ref, o_ref):
    c = c_ref[...]
    o_ref[0] = jnp.dot(_silu(c), w_ref[0], precision=lax.Precision.HIGHEST,
                       preferred_element_type=F32) + b_ref[0]


def _modulation(c, ada_w, ada_b):
    L, D, N = ada_w.shape
    B = c.shape[0]
    tn = 512
    return pl.pallas_call(
        _mod_kernel,
        grid=(L, N // tn),
        in_specs=[pl.BlockSpec((B, D), lambda l, n: (0, 0)),
                  pl.BlockSpec((1, D, tn), lambda l, n: (l, 0, n)),
                  pl.BlockSpec((1, 1, tn), lambda l, n: (l, 0, n))],
        out_specs=pl.BlockSpec((1, B, tn), lambda l, n: (l, 0, n)),
        out_shape=jax.ShapeDtypeStruct((L, B, N), F32),
        compiler_params=_cparams("parallel", "parallel"),
        name="adaln_modulation",
    )(c, ada_w, ada_b.reshape(L, 1, N))


def _wprep_kernel(w_ref, o_ref):
    tail = N_IN - REF_TAIL
    o_ref[0, :, 0:tail] = w_ref[0, :, REF_TAIL:N_IN].astype(BF16)
    o_ref[0, :, tail:tail + REF_ALIGNED] = w_ref[0, :, 0:REF_ALIGNED].astype(BF16)
    kw = w_ref[0, :, REF_ALIGNED:REF_ALIGNED + LANES]
    lane = lax.broadcasted_iota(jnp.int32, kw.shape, 1)
    o_ref[0, :, COL_IX_KW:COL_IX_KW + LANES] = jnp.where(lane < REF_TAIL - REF_ALIGNED, kw, 0.0).astype(BF16)
    o_ref[0, :, COL_IX_KW + LANES:N_PROJ] = jnp.zeros((kw.shape[0], N_PROJ - COL_IX_KW - LANES), BF16)


def _permute_w_in(w_in):
    L, D, N = w_in.shape
    tr = 128
    return pl.pallas_call(
        _wprep_kernel,
        grid=(L, D // tr),
        in_specs=[pl.BlockSpec((1, tr, N), lambda l, r: (l, r, 0))],
        out_specs=pl.BlockSpec((1, tr, N_PROJ), lambda l, r: (l, r, 0)),
        out_shape=jax.ShapeDtypeStruct((L, D, N_PROJ), BF16),
        compiler_params=_cparams("parallel", "parallel"),
        name="w_in_prepare",
    )(w_in)


def _proj_kernel(x_ref, g_ref, shift_ref, scale_ref, w_ref, o_ref, f_ref, u_ref, h_ref):
    j = pl.program_id(1)

    @pl.when(j == 0)
    def _():
        x = x_ref[...]
        r = lax.rsqrt(jnp.mean(x * x, axis=-1, keepdims=True) + EPS)
        h = (x * r) * g_ref[...] * (1.0 + scale_ref[0]) + shift_ref[0]
        h_ref[...] = h.astype(BF16)

    acc = jnp.dot(h_ref[...], w_ref[0], preferred_element_type=F32)
    o_ref[...] = acc.astype(BF16)

    @pl.when(j == COL_HG_F // PROJ_TN)
    def _():
        off = COL_HG_F % PROJ_TN
        f_ref[...] = acc[:, off:off + BRANCH_WIDTH]

    @pl.when(j == COL_S5_U // PROJ_TN)
    def _():
        off = COL_S5_U % PROJ_TN
        u_ref[...] = acc[:, off:off + BRANCH_WIDTH]


def _projection(x2, norm_g, shift, scale, w_all, layer, seq):
    M, D = x2.shape
    tm = min(1024, seq)
    per_batch = seq // tm
    return pl.pallas_call(
        _proj_kernel,
        grid=(M // tm, N_PROJ // PROJ_TN),
        in_specs=[pl.BlockSpec((tm, D), lambda i, j: (i, 0)),
                  pl.BlockSpec((1, D), lambda i, j: (0, 0)),
                  pl.BlockSpec((1, 1, D), lambda i, j: (i // per_batch, 0, 0)),
                  pl.BlockSpec((1, 1, D), lambda i, j: (i // per_batch, 0, 0)),
                  pl.BlockSpec((1, D, PROJ_TN), lambda i, j: (layer, 0, j))],
        out_specs=[pl.BlockSpec((tm, PROJ_TN), lambda i, j: (i, j)),
                   pl.BlockSpec((tm, BRANCH_WIDTH), lambda i, j: (i, 0)),
                   pl.BlockSpec((tm, BRANCH_WIDTH), lambda i, j: (i, 0))],
        out_shape=[jax.ShapeDtypeStruct((M, N_PROJ), BF16),
                   jax.ShapeDtypeStruct((M, BRANCH_WIDTH), F32),
                   jax.ShapeDtypeStruct((M, BRANCH_WIDTH), F32)],
        scratch_shapes=[pltpu.VMEM((tm, D), BF16)],
        compiler_params=_cparams("parallel", "arbitrary"),
        name="norm_in_proj",
    )(x2, norm_g, shift, scale, w_all)


def _split3(x):
    hi = x.astype(BF16)
    r1 = x - hi.astype(F32)
    mid = r1.astype(BF16)
    lo = (r1 - mid.astype(F32)).astype(BF16)
    return hi, mid, lo


def _hgrn_kernel(q_ref, f_ref, i_ref, g_ref, lb_ref, on_ref, o_ref, state_ref, *, rows):
    C, c = HG_CHUNK, HG_SUB
    nsub = C // c

    @pl.when(pl.program_id(2) == 0)
    def _():
        state_ref[...] = jnp.zeros_like(state_ref)

    lb = lb_ref[...]
    row_id = lax.broadcasted_iota(jnp.int32, (C, C), 0)
    col_id = lax.broadcasted_iota(jnp.int32, (C, C), 1)
    tril = jnp.where(col_id <= row_id, 1.0, 0.0).astype(BF16)
    sub_row = lax.broadcasted_iota(jnp.int32, (c, C), 0)
    sub_col = lax.broadcasted_iota(jnp.int32, (c, C), 1)

    for ci in range(rows // C):
        sl = slice(ci * C, (ci + 1) * C)
        q = q_ref[sl, :].astype(F32)
        v16 = i_ref[sl, :]
        fg = lb + (1.0 - lb) * _sigmoid(f_ref[sl, :])
        lf = jnp.log2(jnp.maximum(fg, 1e-30))
        k = 1.0 - fg
        hi, mid, lo = _split3(lf)
        b = (jnp.dot(tril, hi, preferred_element_type=F32)
             + jnp.dot(tril, mid, preferred_element_type=F32)
             + jnp.dot(tril, lo, preferred_element_type=F32))
        b_last = b[C - 1:C, :]

        state_t = state_ref[...]
        q0 = (q * jnp.exp2(b)).astype(BF16)
        o = lax.dot_general(q0, state_t.astype(BF16), _NT, preferred_element_type=F32)

        row_blocks = []
        k_run = []
        for t in range(nsub):
            r0 = t * c
            bt = b[r0:r0 + c, :]
            qt = q[r0:r0 + c, :]
            diag = jnp.zeros((c, C), F32)
            for s in range(c):
                e = jnp.exp2(bt - b[r0 + s:r0 + s + 1, :])
                col = jnp.sum(qt * e * k[r0 + s:r0 + s + 1, :], axis=-1, keepdims=True)
                diag = jnp.where(sub_col == r0 + s, col, diag)
            diag = jnp.where(sub_col - r0 <= sub_row, diag, 0.0)
            if t == 0:
                row_blocks.append(diag)
                continue
            r_t = b[r0 - 1:r0, :]
            if k_run:
                step = jnp.exp2(r_t - b[r0 - c - 1:r0 - c, :])
                k_run = [kb * step for kb in k_run]
            k_run.append(k[r0 - c:r0, :] * jnp.exp2(r_t - b[r0 - c:r0, :]))
            k_t = jnp.concatenate(k_run + [jnp.zeros((C - r0, C), F32)], axis=0).astype(BF16)
            q_t = (qt * jnp.exp2(bt - r_t)).astype(BF16)
            off = lax.dot_general(q_t, k_t, _NT, preferred_element_type=F32)
            row_blocks.append(off + diag)
        scores = jnp.concatenate(row_blocks, axis=0)
        o = o + jnp.dot(scores.astype(BF16), v16, preferred_element_type=F32)

        k_dec = (k * jnp.exp2(b_last - b)).astype(BF16)
        state_ref[...] = (state_t * jnp.exp2(b_last)
                          + lax.dot_general(v16, k_dec, _TN, preferred_element_type=F32))

        r = lax.rsqrt(jnp.mean(o * o, axis=-1, keepdims=True) + EPS)
        o_ref[sl, :] = ((o * r) * on_ref[...] * _silu(g_ref[sl, :].astype(F32))).astype(BF16)


def _hgrn(proj, hgf, lb, onorm_g, batch, seq):
    rows = min(2048, seq)
    nr = seq // rows
    cb = lambda base: (lambda b, h, r: (b * nr + r, base // HG_DIM + h))
    return pl.pallas_call(
        functools.partial(_hgrn_kernel, rows=rows),
        grid=(batch, HG_HEADS, nr),
        in_specs=[pl.BlockSpec((rows, HG_DIM), cb(COL_HG_Q)),
                  pl.BlockSpec((rows, HG_DIM), cb(0)),
                  pl.BlockSpec((rows, HG_DIM), cb(COL_HG_I)),
                  pl.BlockSpec((rows, HG_DIM), cb(COL_HG_G)),
                  pl.BlockSpec((1, HG_DIM), lambda b, h, r: (0, h)),
                  pl.BlockSpec((1, HG_DIM), lambda b, h, r: (0, 0))],
        out_specs=pl.BlockSpec((rows, HG_DIM), lambda b, h, r: (b * nr + r, h)),
        out_shape=jax.ShapeDtypeStruct((batch * seq, BRANCH_WIDTH), BF16),
        scratch_shapes=[pltpu.VMEM((HG_DIM, HG_DIM), F32)],
        compiler_params=_cparams("parallel", "parallel", "arbitrary"),
        name="hgrn2_mixer",
    )(proj, hgf, proj, proj, lb, onorm_g)


def _rope_tables(positions):
    pos = positions.astype(F32)[..., None]

    def tables(dim, reps):
        inv = ROPE_THETA ** (-jnp.arange(0, dim, 2, dtype=F32) / dim)
        ang = pos * inv
        c, s = jnp.cos(ang), jnp.sin(ang)
        return (jnp.concatenate([c, c] * reps, axis=-1),
                jnp.concatenate([-s, s] * reps, axis=-1))

    return tables(AT_DIM, 1) + tables(IDX_DIM, LANES // IDX_DIM)


def _rope_full(x, cos, sin_signed):
    return x * cos + pltpu.roll(x, AT_DIM // 2, 1) * sin_signed


def _rope_idx(x, cos, sin_signed, first_half):
    h = IDX_DIM // 2
    partner = jnp.where(first_half, pltpu.roll(x, LANES - h, 1), pltpu.roll(x, h, 1))
    return x * cos + partner * sin_signed


def _col_reduce(x, op):
    part = 64
    if x.shape[0] > part:
        x = op(x.reshape(x.shape[0] // part, part, x.shape[1]), axis=0)
    return op(x, axis=0, keepdims=True)


def _dsa_block(width, j, q_ref, g_refs, iq_refs, w_rows, ca, sa, ci, si, qn_ref, o_ref,
               k_s, vt_s, ik_s, key_s, val_s, *, topk, first_half, low_group):
    QB = Q_BLOCK
    kk_s, ikk_s = k_s.at[0:width, :], ik_s.at[0:width, :]
    key_w, val_w = key_s.at[0:width, :], val_s.at[0:width, :]

    isc = jnp.zeros((width, QB), F32)
    per_slab = LANES // IDX_DIM
    for m in range(IDX_HEADS // per_slab):
        iq_ref = iq_refs[m // 2]
        off = (m % 2) * LANES
        xr = _rope_idx(iq_ref[:, off:off + LANES].astype(F32), ci, si, first_half)
        for par in range(per_slab):
            h = m * per_slab + par
            xm = jnp.where(low_group if par == 0 else jnp.logical_not(low_group), xr, 0.0)
            rel = lax.dot_general(ikk_s[...], xm.astype(BF16), _NT, preferred_element_type=F32)
            isc = isc + jnp.maximum(rel, 0.0) * w_rows[IDX_DIM + h:IDX_DIM + h + 1, :]
    qpos = j * QB + lax.broadcasted_iota(jnp.int32, (1, QB), 1)
    kpos = lax.broadcasted_iota(jnp.int32, (width, 1), 0)
    causal = kpos <= qpos
    key_w[...] = jnp.where(causal, isc, NEG_BIG)

    kf = float(topk)
    if width <= topk:
        val_w[...] = jnp.where(causal, 1.0, 0.0)
    else:
        def as_score(image):
            bits = jnp.where(image < 0, image ^ jnp.int32(0x7FFFFFFF), image)
            return pltpu.bitcast(bits, F32)

        def count_ge(cand):
            return _col_reduce(jnp.where(key_w[...] >= as_score(cand), 1.0, 0.0), jnp.sum)

        zero = jnp.zeros((1, QB), jnp.int32)
        ans0 = jnp.where(count_ge(zero) >= kf, zero, jnp.full((1, QB), -2**31, jnp.int32))

        def bisect(i, ans):
            cand = ans + (jnp.int32(1) << (30 - i))
            return jnp.where(count_ge(cand) >= kf, cand, ans)

        thr = as_score(lax.fori_loop(0, 31, bisect, ans0))

        key = key_w[...]
        gt = key > thr
        eq = key == thr
        need = kf - _col_reduce(jnp.where(gt, 1.0, 0.0), jnp.sum)
        rr = lax.broadcasted_iota(jnp.int32, (LANES, LANES), 0)
        cc = lax.broadcasted_iota(jnp.int32, (LANES, LANES), 1)
        lower = jnp.where(cc <= rr, 1.0, 0.0).astype(BF16)
        seen = jnp.zeros((1, QB), F32)
        for cidx in range(width // LANES):
            sl = slice(cidx * LANES, (cidx + 1) * LANES)
            eq_c = jnp.where(eq[sl, :], 1.0, 0.0)
            rank = jnp.dot(lower, eq_c.astype(BF16), preferred_element_type=F32) + seen
            take = jnp.where(gt[sl, :], 1.0, jnp.where(rank <= need, eq_c, 0.0))
            val_w[sl, :] = jnp.where(causal[sl, :], take, 0.0)
            seen = rank[LANES - 1:LANES, :]

    scale = math.log2(math.e) / math.sqrt(AT_DIM)
    valid = val_w[...] > 0.0
    for h in range(AT_HEADS):
        hs = slice(h * AT_DIM, (h + 1) * AT_DIM)
        qh = q_ref[:, hs].astype(F32)
        r = lax.rsqrt(jnp.mean(qh * qh, axis=-1, keepdims=True) + EPS)
        qh = _rope_full((qh * r) * qn_ref[...], ca, sa) * scale
        logits = lax.dot_general(kk_s[...], qh.astype(BF16), _NT, preferred_element_type=F32)
        logits = jnp.where(valid, logits, NEG_BIG)
        p = jnp.exp2(logits - _col_reduce(logits, jnp.max))
        denom = _col_reduce(p, jnp.sum)
        oh_t = jnp.dot(vt_s[:, 0:width], p.astype(BF16), preferred_element_type=F32) / denom
        g_ref = g_refs[h // 2]
        goff = (h % 2) * AT_DIM
        o_ref[:, hs] = (oh_t.T * _silu(g_ref[:, goff:goff + AT_DIM].astype(F32))).astype(BF16)


def _dsa_kernel(q_ref, g0_ref, g1_ref, iq0_ref, iq1_ref, k_ref, v_ref, ikw_ref,
                ca_ref, sa_ref, ci_ref, si_ref, qn_ref, kn_ref, o_ref,
                k_s, vt_s, ik_s, key_s, val_s, *, seq, topk):
    j = pl.program_id(1)
    QB = Q_BLOCK
    lane = lax.broadcasted_iota(jnp.int32, (1, LANES), 1)
    first_half = (lane % IDX_DIM) < (IDX_DIM // 2)
    low_group = lane < IDX_DIM

    @pl.when(j == 0)
    def _():
        kk = k_ref[...].astype(F32)
        r = lax.rsqrt(jnp.mean(kk * kk, axis=-1, keepdims=True) + EPS)
        kk = (kk * r) * kn_ref[...]
        k_s[...] = _rope_full(kk, ca_ref[0], sa_ref[0]).astype(BF16)
        vt_s[...] = v_ref[...].astype(F32).T.astype(BF16)
        ik = _rope_idx(ikw_ref[...].astype(F32), ci_ref[0], si_ref[0], first_half)
        ik_s[...] = jnp.where(low_group, ik, pltpu.roll(ik, IDX_DIM, 1)).astype(BF16)

    rows = pl.ds(pl.multiple_of(j * QB, QB), QB)
    ca, sa = ca_ref[0, rows, :], sa_ref[0, rows, :]
    ci, si = ci_ref[0, rows, :], si_ref[0, rows, :]
    w_rows = ikw_ref[rows, :].astype(F32).T

    group = min(KEY_GROUP, seq)
    per_group = group // QB
    for n in range(1, seq // group + 1):
        @pl.when(j // per_group + 1 == n)
        def _(n=n):
            _dsa_block(n * group, j, q_ref, (g0_ref, g1_ref), (iq0_ref, iq1_ref), w_rows,
                       ca, sa, ci, si, qn_ref, o_ref, k_s, vt_s, ik_s, key_s, val_s,
                       topk=topk, first_half=first_half, low_group=low_group)


def _dsa(proj, tables, qn_g, kn_g, batch, seq):
    nb = seq // Q_BLOCK
    topk = min(TOPK_MAX, seq // 4)
    ca, sa, ci, si = tables
    half = BRANCH_WIDTH // 2
    qrow = lambda width, col: pl.BlockSpec((Q_BLOCK, width), lambda b, j: (b * nb + j, col // width))
    krow = lambda col: pl.BlockSpec((seq, LANES), lambda b, j: (b, col // LANES))
    tab = pl.BlockSpec((1, seq, LANES), lambda b, j: (b, 0, 0))
    vec = pl.BlockSpec((1, LANES), lambda b, j: (0, 0))
    return pl.pallas_call(
        functools.partial(_dsa_kernel, seq=seq, topk=topk),
        grid=(batch, nb),
        in_specs=[qrow(BRANCH_WIDTH, COL_AT_Q),
                  qrow(half, COL_AT_G), qrow(half, COL_AT_G + half),
                  qrow(half, COL_IX_Q), qrow(half, COL_IX_Q + half),
                  krow(COL_AT_K), krow(COL_AT_V), krow(COL_IX_KW),
                  tab, tab, tab, tab, vec, vec],
        out_specs=pl.BlockSpec((Q_BLOCK, BRANCH_WIDTH), lambda b, j: (b * nb + j, 0)),
        out_shape=jax.ShapeDtypeStruct((batch * seq, BRANCH_WIDTH), BF16),
        scratch_shapes=[pltpu.VMEM((seq, LANES), BF16), pltpu.VMEM((LANES, seq), BF16),
                        pltpu.VMEM((seq, LANES), BF16),
                        pltpu.VMEM((seq, Q_BLOCK), F32), pltpu.VMEM((seq, Q_BLOCK), F32)],
        compiler_params=_cparams("parallel", "arbitrary"),
        name="dsa_mixer",
    )(proj, proj, proj, proj, proj, proj, proj, proj, ca, sa, ci, si, qn_g, kn_g)


def _s5prep_kernel(are_ref, aim_ref, ldt_ref, btr_ref, bti_ref, cr_ref, ci_ref, perm_ref,
                   t_ref, min_ref, nout_ref, al_ref):
    Lb, G16 = S5_BLOCK, S5_GROUP
    hp = lax.Precision.HIGHEST
    a_re, a_im = are_ref[0], aim_ref[0]
    dt = jnp.exp(ldt_ref[0])
    mag = jnp.exp(a_re * dt)
    ang = a_im * dt
    ab_r, ab_i = mag * jnp.cos(ang), mag * jnp.sin(ang)
    nr, ni = ab_r - 1.0, ab_i
    den = a_re * a_re + a_im * a_im
    fr = (nr * a_re + ni * a_im) / den
    fi = (ni * a_re - nr * a_im) / den
    bt_r, bt_i = btr_ref[0], bti_ref[0]
    bb_r = fr * bt_r - fi * bt_i
    bb_i = fr * bt_i + fi * bt_r
    c_r, c_i = cr_ref[0], ci_ref[0]

    pw_r, pw_i = [jnp.ones_like(ab_r)], [jnp.zeros_like(ab_r)]
    for _ in range(Lb):
        pr, pi = pw_r[-1], pw_i[-1]
        pw_r.append(pr * ab_r - pi * ab_i)
        pw_i.append(pr * ab_i + pi * ab_r)

    def readout(first):
        re = jnp.concatenate([c_r * pw_r[first + t] - c_i * pw_i[first + t] for t in range(Lb)], axis=0)
        im = jnp.concatenate([c_r * pw_i[first + t] + c_i * pw_r[first + t] for t in range(Lb)], axis=0)
        return re, im

    wc_r, wc_i = readout(0)
    taps = (lax.dot_general(bb_r, wc_r, _NT, precision=hp, preferred_element_type=F32)
            - lax.dot_general(bb_i, wc_i, _NT, precision=hp, preferred_element_type=F32))
    t_mat = jnp.concatenate(
        [taps] + [jnp.concatenate([jnp.zeros((G16, s * G16), F32), taps[:, :(Lb - s) * G16]], axis=1)
                  for s in range(1, Lb)], axis=0)
    m_r = jnp.concatenate([pw_r[Lb - 1 - s] * bb_r - pw_i[Lb - 1 - s] * bb_i for s in range(Lb)], axis=0)
    m_i = jnp.concatenate([pw_r[Lb - 1 - s] * bb_i + pw_i[Lb - 1 - s] * bb_r for s in range(Lb)], axis=0)
    m_in = jnp.concatenate([m_r, m_i, m_i, m_r], axis=1)
    n_r, n_i = readout(1)
    n_out = jnp.concatenate([n_r, -n_i], axis=1)

    perm = perm_ref[0]
    t_rows = jnp.dot(perm, t_mat.astype(BF16), preferred_element_type=F32).astype(BF16)
    t_ref[0] = lax.dot_general(t_rows, perm, _NT, preferred_element_type=F32).astype(BF16)
    min_ref[0] = jnp.dot(perm, m_in.astype(BF16), preferred_element_type=F32).astype(BF16)
    nout_ref[0] = jnp.dot(perm, n_out.astype(BF16), preferred_element_type=F32).astype(BF16)
    al_r, al_i = pw_r[Lb], pw_i[Lb]
    al_ref[0] = jnp.concatenate([jnp.concatenate([al_r, al_r], axis=1),
                                 jnp.concatenate([-al_i, al_i], axis=1),
                                 jnp.zeros((6, 2 * S5_STATE), F32)], axis=0)


def _s5_prepare(a_re, a_im, log_dt, b_re, b_im, c_re, c_im):
    G, P, K = S5_GROUPS, S5_STATE, S5_BLOCK * S5_GROUP
    half = LANES // S5_GROUP
    g3 = lambda shape: pl.BlockSpec((1,) + shape, lambda g: (g, 0, 0))
    return pl.pallas_call(
        _s5prep_kernel,
        grid=(G,),
        in_specs=[g3((1, P)), g3((1, P)), g3((1, 1)), g3((S5_GROUP, P)), g3((S5_GROUP, P)),
                  g3((S5_GROUP, P)), g3((S5_GROUP, P)),
                  pl.BlockSpec((1, K, K), lambda g: (g % half, 0, 0))],
        out_specs=[g3((K, K)), g3((K, 4 * P)), g3((K, 2 * P)), g3((8, 2 * P))],
        out_shape=[jax.ShapeDtypeStruct((G, K, K), BF16), jax.ShapeDtypeStruct((G, K, 4 * P), BF16),
                   jax.ShapeDtypeStruct((G, K, 2 * P), BF16), jax.ShapeDtypeStruct((G, 8, 2 * P), F32)],
        compiler_params=_cparams("parallel"),
        name="s5_prepare",
    )(a_re.reshape(G, 1, P), a_im.reshape(G, 1, P), log_dt.reshape(G, 1, 1),
      jnp.swapaxes(b_re, 1, 2), jnp.swapaxes(b_im, 1, 2), c_re, c_im, jnp.asarray(_s5_slot_perm(), BF16))


def _s5_slot(step, g):
    half = LANES // S5_GROUP
    return (step // half) * half + (g % half + step % half) % half


def _s5_slot_perm():
    half = LANES // S5_GROUP
    perm = np.zeros((half, S5_BLOCK * S5_GROUP, S5_BLOCK * S5_GROUP), np.float32)
    for g in range(half):
        for step in range(S5_BLOCK):
            for ch in range(S5_GROUP):
                perm[g, _s5_slot(step, g) * S5_GROUP + ch, step * S5_GROUP + ch] = 1.0
    return perm


def _s5_in_kernel(*refs, nblk):
    u_refs, o_ref = refs[:-1], refs[-1]
    half = LANES // S5_GROUP
    for k, u_ref in enumerate(u_refs):
        for step in range(S5_BLOCK):
            slab = u_ref[pl.ds(step, nblk, stride=S5_BLOCK), :]
            shift = (step % half) * S5_GROUP
            if shift:
                slab = pltpu.roll(slab, shift, 1)
            slab = slab.astype(BF16)
            for gi in range(half):
                g = k * half + gi
                lo = (_s5_slot(step, g) % half) * S5_GROUP
                base = (step // half) * LANES
                o_ref[g, :, base + lo:base + lo + S5_GROUP] = slab[:, lo:lo + S5_GROUP]


def _s5_out_kernel(*refs, nblk):
    y_ref, o_refs = refs[0], refs[1:]
    half = LANES // S5_GROUP
    slot_of_lane = lax.broadcasted_iota(jnp.int32, (1, LANES), 1) // S5_GROUP
    for k, o_ref in enumerate(o_refs):
        for step in range(S5_BLOCK):
            base = (step // half) * LANES
            slab = None
            for gi in range(half):
                g = k * half + gi
                src = y_ref[g, :, base:base + LANES]
                slab = src if slab is None else jnp.where(slot_of_lane == _s5_slot(step, g) % half, src, slab)
            slab = slab.astype(F32)
            shift = (step % half) * S5_GROUP
            if shift:
                slab = pltpu.roll(slab, LANES - shift, 1)
            o_ref[pl.ds(step, nblk, stride=S5_BLOCK), :] = slab


def _s5_kernel(u_ref, t_ref, min_ref, nout_ref, al_ref, y_ref, inj_s, injx_s, xp_s, *, batch, nblk):
    P2 = 2 * S5_STATE
    u = u_ref[0]
    inj = jnp.dot(u, min_ref[0], preferred_element_type=F32)
    inj_s[...] = inj[:, 0:P2]
    injx_s[...] = inj[:, P2:2 * P2]
    a_same = al_ref[0, 0:1, :]
    a_cross = al_ref[0, 1:2, :]

    def step(jb, carry):
        x, xs = carry
        rows = pl.ds(jb, batch, stride=nblk)
        xp_s[rows, :] = x
        return (x * a_same + xs * a_cross + inj_s[rows, :],
                xs * a_same - x * a_cross + injx_s[rows, :])

    z = jnp.zeros((batch, P2), F32)
    lax.fori_loop(0, nblk, step, (z, z), unroll=4)
    y = jnp.dot(u, t_ref[0], preferred_element_type=F32)
    y = y + lax.dot_general(xp_s[...].astype(BF16), nout_ref[0], _NT, preferred_element_type=F32)
    y_ref[0] = y.astype(BF16)


def _s5_scan(u32, ops, batch, seq):
    t_mat, m_in, n_out, al = ops
    G, P, K = S5_GROUPS, S5_STATE, S5_BLOCK * S5_GROUP
    nblk = seq // S5_BLOCK
    R = nblk * batch
    nslab = BRANCH_WIDTH // LANES
    slab = lambda k: pl.BlockSpec((seq, LANES), lambda b, k=k: (b, k))
    grouped = pl.BlockSpec((G, nblk, K), lambda b: (0, b, 0))
    u = pl.pallas_call(
        functools.partial(_s5_in_kernel, nblk=nblk),
        grid=(batch,),
        in_specs=[slab(k) for k in range(nslab)],
        out_specs=grouped,
        out_shape=jax.ShapeDtypeStruct((G, R, K), BF16),
        compiler_params=_cparams("parallel"),
        name="s5_relayout_in",
    )(*([u32] * nslab))
    g3 = lambda shape: pl.BlockSpec((1,) + shape, lambda g: (g, 0, 0))
    y = pl.pallas_call(
        functools.partial(_s5_kernel, batch=batch, nblk=nblk),
        grid=(G,),
        in_specs=[g3((R, K)), g3((K, K)), g3((K, 4 * P)), g3((K, 2 * P)), g3((8, 2 * P))],
        out_specs=g3((R, K)),
        out_shape=jax.ShapeDtypeStruct((G, R, K), BF16),
        scratch_shapes=[pltpu.VMEM((R, 2 * P), F32), pltpu.VMEM((R, 2 * P), F32),
                        pltpu.VMEM((R, 2 * P), F32)],
        compiler_params=_cparams("parallel"),
        name="s5_scan",
    )(u, t_mat, m_in, n_out, al)
    return pl.pallas_call(
        functools.partial(_s5_out_kernel, nblk=nblk),
        grid=(batch,),
        in_specs=[grouped],
        out_specs=[pl.BlockSpec((seq, LANES), lambda b: (b, 0)) for _ in range(nslab)],
        out_shape=[jax.ShapeDtypeStruct((batch * seq, LANES), F32) for _ in range(nslab)],
        compiler_params=_cparams("parallel"),
        name="s5_relayout_out",
    )(y)


def _merge_kernel(x_ref, gate_ref, ya_ref, yb_ref, y50_ref, y51_ref, y52_ref, y53_ref, u_ref, sg_ref,
                  m0_ref, m1_ref, m2_ref, dsk_ref, gw_ref, gb_ref, wb_ref, wo_ref, o_ref):
    y5 = jnp.concatenate([y50_ref[...], y51_ref[...], y52_ref[...], y53_ref[...]], axis=1)
    y = y5 + dsk_ref[...] * u_ref[...].astype(F32)
    y = jax.nn.gelu(y)
    glu = jnp.dot(y.astype(BF16), gw_ref[...], preferred_element_type=F32) + gb_ref[...]
    yc = (y * _sigmoid(glu) * _silu(sg_ref[...].astype(F32))).astype(BF16)
    merged = jnp.zeros(x_ref.shape, F32)
    for n, (yn, m_ref) in enumerate(((ya_ref[...], m0_ref), (yb_ref[...], m1_ref), (yc, m2_ref))):
        yd = jnp.dot(yn, wb_ref[n], preferred_element_type=F32)
        merged = merged + _sigmoid(m_ref[...].astype(F32)) * yd
    o_ref[...] = x_ref[...] + gate_ref[0] * jnp.dot(merged.astype(BF16), wo_ref[...],
                                                    preferred_element_type=F32)


def _merge(x2, gate, ya, yb, y5, proj, d_skip, glu_w, glu_b, w_branch, w_out, seq):
    M, D = x2.shape
    W = BRANCH_WIDTH
    tm = min(512, seq)
    per_batch = seq // tm
    row = lambda cols, colblk: pl.BlockSpec((tm, cols), lambda i: (i, colblk))
    full = lambda shape: pl.BlockSpec(shape, lambda i: (0,) * len(shape))
    return pl.pallas_call(
        _merge_kernel,
        grid=(M // tm,),
        in_specs=[row(D, 0),
                  pl.BlockSpec((1, 1, D), lambda i: (i // per_batch, 0, 0)),
                  row(W, 0), row(W, 0),
                  row(LANES, 0), row(LANES, 0), row(LANES, 0), row(LANES, 0),
                  row(W, COL_S5_U // W), row(W, COL_S5_G // W),
                  row(D, COL_MERGE // D), row(D, COL_MERGE // D + 1), row(D, COL_MERGE // D + 2),
                  full((1, W)), full((W, W)), full((1, W)), full((N_BRANCH, W, D)), full((D, D))],
        out_specs=row(D, 0),
        out_shape=jax.ShapeDtypeStruct((M, D), F32),
        compiler_params=_cparams("parallel"),
        name="merge_out",
    )(x2, gate, ya, yb, *y5, proj, proj, proj, proj, proj, d_skip, glu_w, glu_b, w_branch, w_out)


def kernel(x, c, positions, ada_w, ada_b, norm_g, w_in, hg_lb_logits, hg_onorm_g, at_qnorm_g,
           at_knorm_g, s5_a_re, s5_a_im, s5_log_dt, s5_b_re, s5_b_im, s5_c_re, s5_c_im, s5_d,
           s5_glu_w, s5_glu_b, w_branch, w_out):
    B, S, D = x.shape
    L = ada_w.shape[0]
    lb_all = _lower_bounds(hg_lb_logits.astype(F32))
    mod = _modulation(c, ada_w, ada_b)
    w_all = _permute_w_in(w_in)
    tables = _rope_tables(positions)
    x2 = x.reshape(B * S, D)
    for l in range(L):
        shift = mod[l, :, None, 0:D]
        scale = mod[l, :, None, D:2 * D]
        gate = mod[l, :, None, 2 * D:3 * D]
        proj, hgf, u32 = _projection(x2, norm_g[l][None, :], shift, scale, w_all, l, S)
        ya = _hgrn(proj, hgf, lb_all[l][None, :], hg_onorm_g[l][None, :], B, S)
        yb = _dsa(proj, tables, at_qnorm_g[l][None, :], at_knorm_g[l][None, :], B, S)
        ops = _s5_prepare(s5_a_re[l], s5_a_im[l], s5_log_dt[l], s5_b_re[l], s5_b_im[l],
                          s5_c_re[l], s5_c_im[l])
        y5 = _s5_scan(u32, ops, B, S)
        x2 = _merge(x2, gate, ya, yb, y5, proj, s5_d[l][None, :], s5_glu_w[l].astype(BF16),
                    s5_glu_b[l][None, :], w_branch[l].astype(BF16), w_out[l].astype(BF16), S)
    return x2.reshape(B, S, D)
```

```python
import functools
import math

import jax
import jax.numpy as jnp
import numpy as np
from jax import lax
from jax.experimental import pallas as pl
from jax.experimental.pallas import tpu as pltpu

F32 = jnp.float32
BF16 = jnp.bfloat16

D_MODEL = 1024
DEPTH = 4
BRANCH_WIDTH = 512
N_BRANCH = 3
EPS = 1e-6
NEG_BIG = -1e30
HG_HEADS = 4
HG_DIM = 128
AT_HEADS = 4
AT_DIM = 128
IDX_HEADS = 8
IDX_DIM = 64
TOPK_MAX = 256
Q_BLOCK = 128
ROPE_THETA = 10000.0
S5_GROUP = 16
S5_GROUPS = BRANCH_WIDTH // S5_GROUP
S5_STATE = 64
N_IN = 8008

LANES = 128
HG_CHUNK = 128
HG_SUB = 8
S5_BLOCK = 16
KEY_GROUP = 256
VMEM_LIMIT = 52 * 1024 * 1024

REF_ALIGNED = 3840
REF_TAIL = 3912
COL_S5_U = 0
COL_S5_G = 512
COL_MERGE = 1024
COL_HG_Q = 4096
COL_HG_F = 4608
COL_HG_I = 5120
COL_HG_G = 5632
COL_AT_Q = 6144
COL_AT_K = 6656
COL_AT_V = 6784
COL_AT_G = 6912
COL_IX_Q = 7424
COL_IX_KW = 7936
N_PROJ = 8192
PROJ_TN = 2048
assert COL_HG_F % PROJ_TN + BRANCH_WIDTH <= PROJ_TN and COL_S5_U % PROJ_TN + BRANCH_WIDTH <= PROJ_TN

_NT = (((1,), (1,)), ((), ()))
_TN = (((0,), (0,)), ((), ()))


def _cparams(*sem):
    return pltpu.CompilerParams(dimension_semantics=sem, vmem_limit_bytes=VMEM_LIMIT)


def _sigmoid(x):
    return 1.0 / (1.0 + jnp.exp(-x))


def _silu(x):
    return x * _sigmoid(x)


def _lb_kernel(z_ref, o_ref):
    z = z_ref[...]
    e = jnp.exp(z - jnp.max(z, axis=0, keepdims=True))
    p = e / jnp.sum(e, axis=0, keepdims=True)
    acc = jnp.zeros_like(p[0:1])
    for l in range(z.shape[0]):
        acc = acc + p[l:l + 1]
        o_ref[l:l + 1, :] = acc - p[0:1]


def _lower_bounds(logits):
    return pl.pallas_call(
        _lb_kernel, out_shape=jax.ShapeDtypeStruct(logits.shape, F32), name="hg_lower_bounds",
    )(logits)


def _mod_kernel(c_ref, w_ref, b_ref, o_ref):
    c = c_ref[...]
    o_ref[0] = jnp.dot(_silu(c), w_ref[0], precision=lax.Precision.HIGHEST,
                       preferred_element_type=F32) + b_ref[0]


def _modulation(c, ada_w, ada_b):
    L, D, N = ada_w.shape
    B = c.shape[0]
    tn = 512
    return pl.pallas_call(
        _mod_kernel,
        grid=(L, N // tn),
        in_specs=[pl.BlockSpec((B, D), lambda l, n: (0, 0)),
                  pl.BlockSpec((1, D, tn), lambda l, n: (l, 0, n)),
                  pl.BlockSpec((1, 1, tn), lambda l, n: (l, 0, n))],
        out_specs=pl.BlockSpec((1, B, tn), lambda l, n: (l, 0, n)),
        out_shape=jax.ShapeDtypeStruct((L, B, N), F32),
        compiler_params=_cparams("parallel", "parallel"),
        name="adaln_modulation",
    )(c, ada_w, ada_b.reshape(L, 1, N))


def _wprep_kernel(w_ref, o_ref):
    tail = N_IN - REF_TAIL
    o_ref[0, :, 0:tail] = w_ref[0, :, REF_TAIL:N_IN].astype(BF16)
    o_ref[0, :, tail:tail + REF_ALIGNED] = w_ref[0, :, 0:REF_ALIGNED].astype(BF16)
    kw = w_ref[0, :, REF_ALIGNED:REF_ALIGNED + LANES]
    lane = lax.broadcasted_iota(jnp.int32, kw.shape, 1)
    o_ref[0, :, COL_IX_KW:COL_IX_KW + LANES] = jnp.where(lane < REF_TAIL - REF_ALIGNED, kw, 0.0).astype(BF16)
    o_ref[0, :, COL_IX_KW + LANES:N_PROJ] = jnp.zeros((kw.shape[0], N_PROJ - COL_IX_KW - LANES), BF16)


def _permute_w_in(w_in):
    L, D, N = w_in.shape
    tr = 128
    return pl.pallas_call(
        _wprep_kernel,
        grid=(L, D // tr),
        in_specs=[pl.BlockSpec((1, tr, N), lambda l, r: (l, r, 0))],
        out_specs=pl.BlockSpec((1, tr, N_PROJ), lambda l, r: (l, r, 0)),
        out_shape=jax.ShapeDtypeStruct((L, D, N_PROJ), BF16),
        compiler_params=_cparams("parallel", "parallel"),
        name="w_in_prepare",
    )(w_in)


def _proj_kernel(x_ref, g_ref, shift_ref, scale_ref, w_ref, o_ref, f_ref, u_ref, h_ref):
    j = pl.program_id(1)

    @pl.when(j == 0)
    def _():
        x = x_ref[...]
        r = lax.rsqrt(jnp.mean(x * x, axis=-1, keepdims=True) + EPS)
        h = (x * r) * g_ref[...] * (1.0 + scale_ref[0]) + shift_ref[0]
        h_ref[...] = h.astype(BF16)

    acc = jnp.dot(h_ref[...], w_ref[0], preferred_element_type=F32)
    o_ref[...] = acc.astype(BF16)

    @pl.when(j == COL_HG_F // PROJ_TN)
    def _():
        off = COL_HG_F % PROJ_TN
        f_ref[...] = acc[:, off:off + BRANCH_WIDTH]

    @pl.when(j == COL_S5_U // PROJ_TN)
    def _():
        off = COL_S5_U % PROJ_TN
        u_ref[...] = acc[:, off:off + BRANCH_WIDTH]


def _projection(x2, norm_g, shift, scale, w_all, layer, seq):
    M, D = x2.shape
    tm = min(1024, seq)
    per_batch = seq // tm
    return pl.pallas_call(
        _proj_kernel,
        grid=(M // tm, N_PROJ // PROJ_TN),
        in_specs=[pl.BlockSpec((tm, D), lambda i, j: (i, 0)),
                  pl.BlockSpec((1, D), lambda i, j: (0, 0)),
                  pl.BlockSpec((1, 1, D), lambda i, j: (i // per_batch, 0, 0)),
                  pl.BlockSpec((1, 1, D), lambda i, j: (i // per_batch, 0, 0)),
                  pl.BlockSpec((1, D, PROJ_TN), lambda i, j: (layer, 0, j))],
        out_specs=[pl.BlockSpec((tm, PROJ_TN), lambda i, j: (i, j)),
                   pl.BlockSpec((tm, BRANCH_WIDTH), lambda i, j: (i, 0)),
                   pl.BlockSpec((tm, BRANCH_WIDTH), lambda i, j: (i, 0))],
        out_shape=[jax.ShapeDtypeStruct((M, N_PROJ), BF16),
                   jax.ShapeDtypeStruct((M, BRANCH_WIDTH), F32),
                   jax.ShapeDtypeStruct((M, BRANCH_WIDTH), F32)],
        scratch_shapes=[pltpu.VMEM((tm, D), BF16)],
        compiler_params=_cparams("parallel", "arbitrary"),
        name="norm_in_proj",
    )(x2, norm_g, shift, scale, w_all)


def _split3(x):
    hi = x.astype(BF16)
    r1 = x - hi.astype(F32)
    mid = r1.astype(BF16)
    lo = (r1 - mid.astype(F32)).astype(BF16)
    return hi, mid, lo


def _hgrn_kernel(q_ref, f_ref, i_ref, g_ref, lb_ref, on_ref, o_ref, state_ref, *, rows):
    C, c = HG_CHUNK, HG_SUB
    nsub = C // c

    @pl.when(pl.program_id(2) == 0)
    def _():
        state_ref[...] = jnp.zeros_like(state_ref)

    lb = lb_ref[...]
    row_id = lax.broadcasted_iota(jnp.int32, (C, C), 0)
    col_id = lax.broadcasted_iota(jnp.int32, (C, C), 1)
    tril = jnp.where(col_id <= row_id, 1.0, 0.0).astype(BF16)
    sub_row = lax.broadcasted_iota(jnp.int32, (c, C), 0)
    sub_col = lax.broadcasted_iota(jnp.int32, (c, C), 1)

    for ci in range(rows // C):
        sl = slice(ci * C, (ci + 1) * C)
        q = q_ref[sl, :].astype(F32)
        v16 = i_ref[sl, :]
        fg = lb + (1.0 - lb) * _sigmoid(f_ref[sl, :])
        lf = jnp.log2(jnp.maximum(fg, 1e-30))
        k = 1.0 - fg
        hi, mid, lo = _split3(lf)
        b = (jnp.dot(tril, hi, preferred_element_type=F32)
             + jnp.dot(tril, mid, preferred_element_type=F32)
             + jnp.dot(tril, lo, preferred_element_type=F32))
        b_last = b[C - 1:C, :]

        state_t = state_ref[...]
        q0 = (q * jnp.exp2(b)).astype(BF16)
        o = lax.dot_general(q0, state_t.astype(BF16), _NT, preferred_element_type=F32)

        row_blocks = []
        k_run = []
        for t in range(nsub):
            r0 = t * c
            bt = b[r0:r0 + c, :]
            qt = q[r0:r0 + c, :]
            diag = jnp.zeros((c, C), F32)
            for s in range(c):
                e = jnp.exp2(bt - b[r0 + s:r0 + s + 1, :])
                col = jnp.sum(qt * e * k[r0 + s:r0 + s + 1, :], axis=-1, keepdims=True)
                diag = jnp.where(sub_col == r0 + s, col, diag)
            diag = jnp.where(sub_col - r0 <= sub_row, diag, 0.0)
            if t == 0:
                row_blocks.append(diag)
                continue
            r_t = b[r0 - 1:r0, :]
            if k_run:
                step = jnp.exp2(r_t - b[r0 - c - 1:r0 - c, :])
                k_run = [kb * step for kb in k_run]
            k_run.append(k[r0 - c:r0, :] * jnp.exp2(r_t - b[r0 - c:r0, :]))
            k_t = jnp.concatenate(k_run + [jnp.zeros((C - r0, C), F32)], axis=0).astype(BF16)
            q_t = (qt * jnp.exp2(bt - r_t)).astype(BF16)
            off = lax.dot_general(q_t, k_t, _NT, preferred_element_type=F32)
            row_blocks.append(off + diag)
        scores = jnp.concatenate(row_blocks, axis=0)
        o = o + jnp.dot(scores.astype(BF16), v16, preferred_element_type=F32)

        k_dec = (k * jnp.exp2(b_last - b)).astype(BF16)
        state_ref[...] = (state_t * jnp.exp2(b_last)
                          + lax.dot_general(v16, k_dec, _TN, preferred_element_type=F32))

        r = lax.rsqrt(jnp.mean(o * o, axis=-1, keepdims=True) + EPS)
        o_ref[sl, :] = ((o * r) * on_ref[...] * _silu(g_ref[sl, :].astype(F32))).astype(BF16)


def _hgrn(proj, hgf, lb, onorm_g, batch, seq):
    rows = min(2048, seq)
    nr = seq // rows
    cb = lambda base: (lambda b, h, r: (b * nr + r, base // HG_DIM + h))
    return pl.pallas_call(
        functools.partial(_hgrn_kernel, rows=rows),
        grid=(batch, HG_HEADS, nr),
        in_specs=[pl.BlockSpec((rows, HG_DIM), cb(COL_HG_Q)),
                  pl.BlockSpec((rows, HG_DIM), cb(0)),
                  pl.BlockSpec((rows, HG_DIM), cb(COL_HG_I)),
                  pl.BlockSpec((rows, HG_DIM), cb(COL_HG_G)),
                  pl.BlockSpec((1, HG_DIM), lambda b, h, r: (0, h)),
                  pl.BlockSpec((1, HG_DIM), lambda b, h, r: (0, 0))],
        out_specs=pl.BlockSpec((rows, HG_DIM), lambda b, h, r: (b * nr + r, h)),
        out_shape=jax.ShapeDtypeStruct((batch * seq, BRANCH_WIDTH), BF16),
        scratch_shapes=[pltpu.VMEM((HG_DIM, HG_DIM), F32)],
        compiler_params=_cparams("parallel", "parallel", "arbitrary"),
        name="hgrn2_mixer",
    )(proj, hgf, proj, proj, lb, onorm_g)


def _rope_tables(positions):
    pos = positions.astype(F32)[..., None]

    def tables(dim, reps):
        inv = ROPE_THETA ** (-jnp.arange(0, dim, 2, dtype=F32) / dim)
        ang = pos * inv
        c, s = jnp.cos(ang), jnp.sin(ang)
        return (jnp.concatenate([c, c] * reps, axis=-1),
                jnp.concatenate([-s, s] * reps, axis=-1))

    return tables(AT_DIM, 1) + tables(IDX_DIM, LANES // IDX_DIM)


def _rope_full(x, cos, sin_signed):
    return x * cos + pltpu.roll(x, AT_DIM // 2, 1) * sin_signed


def _rope_idx(x, cos, sin_signed, first_half):
    h = IDX_DIM // 2
    partner = jnp.where(first_half, pltpu.roll(x, LANES - h, 1), pltpu.roll(x, h, 1))
    return x * cos + partner * sin_signed


def _col_reduce(x, op):
    part = 64
    if x.shape[0] > part:
        x = op(x.reshape(x.shape[0] // part, part, x.shape[1]), axis=0)
    return op(x, axis=0, keepdims=True)


def _dsa_block(width, j, q_ref, g_refs, iq_refs, w_rows, ca, sa, ci, si, qn_ref, o_ref,
               k_s, vt_s, ik_s, key_s, val_s, *, topk, first_half, low_group):
    QB = Q_BLOCK
    kk_s, ikk_s = k_s.at[0:width, :], ik_s.at[0:width, :]
    key_w, val_w = key_s.at[0:width, :], val_s.at[0:width, :]

    isc = jnp.zeros((width, QB), F32)
    per_slab = LANES // IDX_DIM
    for m in range(IDX_HEADS // per_slab):
        iq_ref = iq_refs[m // 2]
        off = (m % 2) * LANES
        xr = _rope_idx(iq_ref[:, off:off + LANES].astype(F32), ci, si, first_half)
        for par in range(per_slab):
            h = m * per_slab + par
            xm = jnp.where(low_group if par == 0 else jnp.logical_not(low_group), xr, 0.0)
            rel = lax.dot_general(ikk_s[...], xm.astype(BF16), _NT, preferred_element_type=F32)
            isc = isc + jnp.maximum(rel, 0.0) * w_rows[IDX_DIM + h:IDX_DIM + h + 1, :]
    qpos = j * QB + lax.broadcasted_iota(jnp.int32, (1, QB), 1)
    kpos = lax.broadcasted_iota(jnp.int32, (width, 1), 0)
    causal = kpos <= qpos
    key_w[...] = jnp.where(causal, isc, NEG_BIG)

    kf = float(topk)
    if width <= topk:
        val_w[...] = jnp.where(causal, 1.0, 0.0)
    else:
        def as_score(image):
            bits = jnp.where(image < 0, image ^ jnp.int32(0x7FFFFFFF), image)
            return pltpu.bitcast(bits, F32)

        def count_ge(cand):
            return _col_reduce(jnp.where(key_w[...] >= as_score(cand), 1.0, 0.0), jnp.sum)

        zero = jnp.zeros((1, QB), jnp.int32)
        ans0 = jnp.where(count_ge(zero) >= kf, zero, jnp.full((1, QB), -2**31, jnp.int32))

        def bisect(i, ans):
            cand = ans + (jnp.int32(1) << (30 - i))
            return jnp.where(count_ge(cand) >= kf, cand, ans)

        thr = as_score(lax.fori_loop(0, 31, bisect, ans0))

        key = key_w[...]
        gt = key > thr
        eq = key == thr
        need = kf - _col_reduce(jnp.where(gt, 1.0, 0.0), jnp.sum)
        rr = lax.broadcasted_iota(jnp.int32, (LANES, LANES), 0)
        cc = lax.broadcasted_iota(jnp.int32, (LANES, LANES), 1)
        lower = jnp.where(cc <= rr, 1.0, 0.0).astype(BF16)
        seen = jnp.zeros((1, QB), F32)
        for cidx in range(width // LANES):
            sl = slice(cidx * LANES, (cidx + 1) * LANES)
            eq_c = jnp.where(eq[sl, :], 1.0, 0.0)
            rank = jnp.dot(lower, eq_c.astype(BF16), preferred_element_type=F32) + seen
            take = jnp.where(gt[sl, :], 1.0, jnp.where(rank <= need, eq_c, 0.0))
            val_w[sl, :] = jnp.where(causal[sl, :], take, 0.0)
            seen = rank[LANES - 1:LANES, :]

    scale = math.log2(math.e) / math.sqrt(AT_DIM)
    valid = val_w[...] > 0.0
    for h in range(AT_HEADS):
        hs = slice(h * AT_DIM, (h + 1) * AT_DIM)
        qh = q_ref[:, hs].astype(F32)
        r = lax.rsqrt(jnp.mean(qh * qh, axis=-1, keepdims=True) + EPS)
        qh = _rope_full((qh * r) * qn_ref[...], ca, sa) * scale
        logits = lax.dot_general(kk_s[...], qh.astype(BF16), _NT, preferred_element_type=F32)
        logits = jnp.where(valid, logits, NEG_BIG)
        p = jnp.exp2(logits - _col_reduce(logits, jnp.max))
        denom = _col_reduce(p, jnp.sum)
        oh_t = jnp.dot(vt_s[:, 0:width], p.astype(BF16), preferred_element_type=F32) / denom
        g_ref = g_refs[h // 2]
        goff = (h % 2) * AT_DIM
        o_ref[:, hs] = (oh_t.T * _silu(g_ref[:, goff:goff + AT_DIM].astype(F32))).astype(BF16)


def _dsa_kernel(q_ref, g0_ref, g1_ref, iq0_ref, iq1_ref, k_ref, v_ref, ikw_ref,
                ca_ref, sa_ref, ci_ref, si_ref, qn_ref, kn_ref, o_ref,
                k_s, vt_s, ik_s, key_s, val_s, *, seq, topk):
    j = pl.program_id(1)
    QB = Q_BLOCK
    lane = lax.broadcasted_iota(jnp.int32, (1, LANES), 1)
    first_half = (lane % IDX_DIM) < (IDX_DIM // 2)
    low_group = lane < IDX_DIM

    @pl.when(j == 0)
    def _():
        kk = k_ref[...].astype(F32)
        r = lax.rsqrt(jnp.mean(kk * kk, axis=-1, keepdims=True) + EPS)
        kk = (kk * r) * kn_ref[...]
        k_s[...] = _rope_full(kk, ca_ref[0], sa_ref[0]).astype(BF16)
        vt_s[...] = v_ref[...].astype(F32).T.astype(BF16)
        ik = _rope_idx(ikw_ref[...].astype(F32), ci_ref[0], si_ref[0], first_half)
        ik_s[...] = jnp.where(low_group, ik, pltpu.roll(ik, IDX_DIM, 1)).astype(BF16)

    rows = pl.ds(pl.multiple_of(j * QB, QB), QB)
    ca, sa = ca_ref[0, rows, :], sa_ref[0, rows, :]
    ci, si = ci_ref[0, rows, :], si_ref[0, rows, :]
    w_rows = ikw_ref[rows, :].astype(F32).T

    group = min(KEY_GROUP, seq)
    per_group = group // QB
    for n in range(1, seq // group + 1):
        @pl.when(j // per_group + 1 == n)
        def _(n=n):
            _dsa_block(n * group, j, q_ref, (g0_ref, g1_ref), (iq0_ref, iq1_ref), w_rows,
                       ca, sa, ci, si, qn_ref, o_ref, k_s, vt_s, ik_s, key_s, val_s,
                       topk=topk, first_half=first_half, low_group=low_group)


def _dsa(proj, tables, qn_g, kn_g, batch, seq):
    nb = seq // Q_BLOCK
    topk = min(TOPK_MAX, seq // 4)
    ca, sa, ci, si = tables
    half = BRANCH_WIDTH // 2
    qrow = lambda width, col: pl.BlockSpec((Q_BLOCK, width), lambda b, j: (b * nb + j, col // width))
    krow = lambda col: pl.BlockSpec((seq, LANES), lambda b, j: (b, col // LANES))
    tab = pl.BlockSpec((1, seq, LANES), lambda b, j: (b, 0, 0))
    vec = pl.BlockSpec((1, LANES), lambda b, j: (0, 0))
    return pl.pallas_call(
        functools.partial(_dsa_kernel, seq=seq, topk=topk),
        grid=(batch, nb),
        in_specs=[qrow(BRANCH_WIDTH, COL_AT_Q),
                  qrow(half, COL_AT_G), qrow(half, COL_AT_G + half),
                  qrow(half, COL_IX_Q), qrow(half, COL_IX_Q + half),
                  krow(COL_AT_K), krow(COL_AT_V), krow(COL_IX_KW),
                  tab, tab, tab, tab, vec, vec],
        out_specs=pl.BlockSpec((Q_BLOCK, BRANCH_WIDTH), lambda b, j: (b * nb + j, 0)),
        out_shape=jax.ShapeDtypeStruct((batch * seq, BRANCH_WIDTH), BF16),
        scratch_shapes=[pltpu.VMEM((seq, LANES), BF16), pltpu.VMEM((LANES, seq), BF16),
                        pltpu.VMEM((seq, LANES), BF16),
                        pltpu.VMEM((seq, Q_BLOCK), F32), pltpu.VMEM((seq, Q_BLOCK), F32)],
        compiler_params=_cparams("parallel", "arbitrary"),
        name="dsa_mixer",
    )(proj, proj, proj, proj, proj, proj, proj, proj, ca, sa, ci, si, qn_g, kn_g)


def _s5prep_kernel(are_ref, aim_ref, ldt_ref, btr_ref, bti_ref, cr_ref, ci_ref, perm_ref,
                   t_ref, min_ref, nout_ref, al_ref):
    Lb, G16 = S5_BLOCK, S5_GROUP
    hp = lax.Precision.HIGHEST
    a_re, a_im = are_ref[0], aim_ref[0]
    dt = jnp.exp(ldt_ref[0])
    mag = jnp.exp(a_re * dt)
    ang = a_im * dt
    ab_r, ab_i = mag * jnp.cos(ang), mag * jnp.sin(ang)
    nr, ni = ab_r - 1.0, ab_i
    den = a_re * a_re + a_im * a_im
    fr = (nr * a_re + ni * a_im) / den
    fi = (ni * a_re - nr * a_im) / den
    bt_r, bt_i = btr_ref[0], bti_ref[0]
    bb_r = fr * bt_r - fi * bt_i
    bb_i = fr * bt_i + fi * bt_r
    c_r, c_i = cr_ref[0], ci_ref[0]

    pw_r, pw_i = [jnp.ones_like(ab_r)], [jnp.zeros_like(ab_r)]
    for _ in range(Lb):
        pr, pi = pw_r[-1], pw_i[-1]
        pw_r.append(pr * ab_r - pi * ab_i)
        pw_i.append(pr * ab_i + pi * ab_r)

    def readout(first):
        re = jnp.concatenate([c_r * pw_r[first + t] - c_i * pw_i[first + t] for t in range(Lb)], axis=0)
        im = jnp.concatenate([c_r * pw_i[first + t] + c_i * pw_r[first + t] for t in range(Lb)], axis=0)
        return re, im

    wc_r, wc_i = readout(0)
    taps = (lax.dot_general(bb_r, wc_r, _NT, precision=hp, preferred_element_type=F32)
            - lax.dot_general(bb_i, wc_i, _NT, precision=hp, preferred_element_type=F32))
    t_mat = jnp.concatenate(
        [taps] + [jnp.concatenate([jnp.zeros((G16, s * G16), F32), taps[:, :(Lb - s) * G16]], axis=1)
                  for s in range(1, Lb)], axis=0)
    m_r = jnp.concatenate([pw_r[Lb - 1 - s] * bb_r - pw_i[Lb - 1 - s] * bb_i for s in range(Lb)], axis=0)
    m_i = jnp.concatenate([pw_r[Lb - 1 - s] * bb_i + pw_i[Lb - 1 - s] * bb_r for s in range(Lb)], axis=0)
    m_in = jnp.concatenate([m_r, m_i, m_i, m_r], axis=1)
    n_r, n_i = readout(1)
    n_out = jnp.concatenate([n_r, -n_i], axis=1)

    perm = perm_ref[0]
    t_rows = jnp.dot(perm, t_mat.astype(BF16), preferred_element_type=F32).astype(BF16)
    t_ref[0] = lax.dot_general(t_rows, perm, _NT, preferred_element_type=F32).astype(BF16)
    min_ref[0] = jnp.dot(perm, m_in.astype(BF16), preferred_element_type=F32).astype(BF16)
    nout_ref[0] = jnp.dot(perm, n_out.astype(BF16), preferred_element_type=F32).astype(BF16)
    al_r, al_i = pw_r[Lb], pw_i[Lb]
    al_ref[0] = jnp.concatenate([jnp.concatenate([al_r, al_r], axis=1),
                                 jnp.concatenate([-al_i, al_i], axis=1),
                                 jnp.zeros((6, 2 * S5_STATE), F32)], axis=0)


def _s5_prepare(a_re, a_im, log_dt, b_re, b_im, c_re, c_im):
    G, P, K = S5_GROUPS, S5_STATE, S5_BLOCK * S5_GROUP
    half = LANES // S5_GROUP
    g3 = lambda shape: pl.BlockSpec((1,) + shape, lambda g: (g, 0, 0))
    return pl.pallas_call(
        _s5prep_kernel,
        grid=(G,),
        in_specs=[g3((1, P)), g3((1, P)), g3((1, 1)), g3((S5_GROUP, P)), g3((S5_GROUP, P)),
                  g3((S5_GROUP, P)), g3((S5_GROUP, P)),
                  pl.BlockSpec((1, K, K), lambda g: (g % half, 0, 0))],
        out_specs=[g3((K, K)), g3((K, 4 * P)), g3((K, 2 * P)), g3((8, 2 * P))],
        out_shape=[jax.ShapeDtypeStruct((G, K, K), BF16), jax.ShapeDtypeStruct((G, K, 4 * P), BF16),
                   jax.ShapeDtypeStruct((G, K, 2 * P), BF16), jax.ShapeDtypeStruct((G, 8, 2 * P), F32)],
        compiler_params=_cparams("parallel"),
        name="s5_prepare",
    )(a_re.reshape(G, 1, P), a_im.reshape(G, 1, P), log_dt.reshape(G, 1, 1),
      jnp.swapaxes(b_re, 1, 2), jnp.swapaxes(b_im, 1, 2), c_re, c_im, jnp.asarray(_s5_slot_perm(), BF16))


def _s5_slot(step, g):
    half = LANES // S5_GROUP
    return (step // half) * half + (g % half + step % half) % half


def _s5_slot_perm():
    half = LANES // S5_GROUP
    perm = np.zeros((half, S5_BLOCK * S5_GROUP, S5_BLOCK * S5_GROUP), np.float32)
    for g in range(half):
        for step in range(S5_BLOCK):
            for ch in range(S5_GROUP):
                perm[g, _s5_slot(step, g) * S5_GROUP + ch, step * S5_GROUP + ch] = 1.0
    return perm


def _s5_in_kernel(*refs, nblk):
    u_refs, o_ref = refs[:-1], refs[-1]
    half = LANES // S5_GROUP
    for k, u_ref in enumerate(u_refs):
        for step in range(S5_BLOCK):
            slab = u_ref[pl.ds(step, nblk, stride=S5_BLOCK), :]
            shift = (step % half) * S5_GROUP
            if shift:
                slab = pltpu.roll(slab, shift, 1)
            slab = slab.astype(BF16)
            for gi in range(half):
                g = k * half + gi
                lo = (_s5_slot(step, g) % half) * S5_GROUP
                base = (step // half) * LANES
                o_ref[g, :, base + lo:base + lo + S5_GROUP] = slab[:, lo:lo + S5_GROUP]


def _s5_out_kernel(*refs, nblk):
    y_ref, o_refs = refs[0], refs[1:]
    half = LANES // S5_GROUP
    slot_of_lane = lax.broadcasted_iota(jnp.int32, (1, LANES), 1) // S5_GROUP
    for k, o_ref in enumerate(o_refs):
        for step in range(S5_BLOCK):
            base = (step // half) * LANES
            slab = None
            for gi in range(half):
                g = k * half + gi
                src = y_ref[g, :, base:base + LANES]
                slab = src if slab is None else jnp.where(slot_of_lane == _s5_slot(step, g) % half, src, slab)
            slab = slab.astype(F32)
            shift = (step % half) * S5_GROUP
            if shift:
                slab = pltpu.roll(slab, LANES - shift, 1)
            o_ref[pl.ds(step, nblk, stride=S5_BLOCK), :] = slab


def _s5_kernel(u_ref, t_ref, min_ref, nout_ref, al_ref, y_ref, inj_s, injx_s, xp_s, *, batch, nblk):
    P2 = 2 * S5_STATE
    u = u_ref[0]
    inj = jnp.dot(u, min_ref[0], preferred_element_type=F32)
    inj_s[...] = inj[:, 0:P2]
    injx_s[...] = inj[:, P2:2 * P2]
    a_same = al_ref[0, 0:1, :]
    a_cross = al_ref[0, 1:2, :]

    def step(jb, carry):
        x, xs = carry
        rows = pl.ds(jb, batch, stride=nblk)
        xp_s[rows, :] = x
        return (x * a_same + xs * a_cross + inj_s[rows, :],
                xs * a_same - x * a_cross + injx_s[rows, :])

    z = jnp.zeros((batch, P2), F32)
    lax.fori_loop(0, nblk, step, (z, z), unroll=4)
    y = jnp.dot(u, t_ref[0], preferred_element_type=F32)
    y = y + lax.dot_general(xp_s[...].astype(BF16), nout_ref[0], _NT, preferred_element_type=F32)
    y_ref[0] = y.astype(BF16)


def _s5_scan(u32, ops, batch, seq):
    t_mat, m_in, n_out, al = ops
    G, P, K = S5_GROUPS, S5_STATE, S5_BLOCK * S5_GROUP
    nblk = seq // S5_BLOCK
    R = nblk * batch
    nslab = BRANCH_WIDTH // LANES
    slab = lambda k: pl.BlockSpec((seq, LANES), lambda b, k=k: (b, k))
    grouped = pl.BlockSpec((G, nblk, K), lambda b: (0, b, 0))
    u = pl.pallas_call(
        functools.partial(_s5_in_kernel, nblk=nblk),
        grid=(batch,),
        in_specs=[slab(k) for k in range(nslab)],
        out_specs=grouped,
        out_shape=jax.ShapeDtypeStruct((G, R, K), BF16),
        compiler_params=_cparams("parallel"),
        name="s5_relayout_in",
    )(*([u32] * nslab))
    g3 = lambda shape: pl.BlockSpec((1,) + shape, lambda g: (g, 0, 0))
    y = pl.pallas_call(
        functools.partial(_s5_kernel, batch=batch, nblk=nblk),
        grid=(G,),
        in_specs=[g3((R, K)), g3((K, K)), g3((K, 4 * P)), g3((K, 2 * P)), g3((8, 2 * P))],
        out_specs=g3((R, K)),
        out_shape=jax.ShapeDtypeStruct((G, R, K), BF16),
        scratch_shapes=[pltpu.VMEM((R, 2 * P), F32), pltpu.VMEM((R, 2 * P), F32),
                        pltpu.VMEM((R, 2 * P), F32)],
        compiler_params=_cparams("parallel"),
        name="s5_scan",
    )(u, t_mat, m_in, n_out, al)
    return pl.pallas_call(
        functools.partial(_s5_out_kernel, nblk=nblk),
        grid=(batch,),
        in_specs=[grouped],
        out_specs=[pl.BlockSpec((seq, LANES), lambda b: (b, 0)) for _ in range(nslab)],
        out_shape=[jax.ShapeDtypeStruct((batch * seq, LANES), F32) for _ in range(nslab)],
        compiler_params=_cparams("parallel"),
        name="s5_relayout_out",
    )(y)


def _merge_kernel(x_ref, gate_ref, ya_ref, yb_ref, y50_ref, y51_ref, y52_ref, y53_ref, u_ref, sg_ref,
                  m0_ref, m1_ref, m2_ref, dsk_ref, gw_ref, gb_ref, wb_ref, wo_ref, o_ref):
    y5 = jnp.concatenate([y50_ref[...], y51_ref[...], y52_ref[...], y53_ref[...]], axis=1)
    y = y5 + dsk_ref[...] * u_ref[...].astype(F32)
    y = jax.nn.gelu(y)
    glu = jnp.dot(y.astype(BF16), gw_ref[...], preferred_element_type=F32) + gb_ref[...]
    yc = (y * _sigmoid(glu) * _silu(sg_ref[...].astype(F32))).astype(BF16)
    merged = jnp.zeros(x_ref.shape, F32)
    for n, (yn, m_ref) in enumerate(((ya_ref[...], m0_ref), (yb_ref[...], m1_ref), (yc, m2_ref))):
        yd = jnp.dot(yn, wb_ref[n], preferred_element_type=F32)
        merged = merged + _sigmoid(m_ref[...].astype(F32)) * yd
    o_ref[...] = x_ref[...] + gate_ref[0] * jnp.dot(merged.astype(BF16), wo_ref[...],
                                                    preferred_element_type=F32)


def _merge(x2, gate, ya, yb, y5, proj, d_skip, glu_w, glu_b, w_branch, w_out, seq):
    M, D = x2.shape
    W = BRANCH_WIDTH
    tm = min(512, seq)
    per_batch = seq // tm
    row = lambda cols, colblk: pl.BlockSpec((tm, cols), lambda i: (i, colblk))
    full = lambda shape: pl.BlockSpec(shape, lambda i: (0,) * len(shape))
    return pl.pallas_call(
        _merge_kernel,
        grid=(M // tm,),
        in_specs=[row(D, 0),
                  pl.BlockSpec((1, 1, D), lambda i: (i // per_batch, 0, 0)),
                  row(W, 0), row(W, 0),
                  row(LANES, 0), row(LANES, 0), row(LANES, 0), row(LANES, 0),
                  row(W, COL_S5_U // W), row(W, COL_S5_G // W),
                  row(D, COL_MERGE // D), row(D, COL_MERGE // D + 1), row(D, COL_MERGE // D + 2),
                  full((1, W)), full((W, W)), full((1, W)), full((N_BRANCH, W, D)), full((D, D))],
        out_specs=row(D, 0),
        out_shape=jax.ShapeDtypeStruct((M, D), F32),
        compiler_params=_cparams("parallel"),
        name="merge_out",
    )(x2, gate, ya, yb, *y5, proj, proj, proj, proj, proj, d_skip, glu_w, glu_b, w_branch, w_out)


def kernel(x, c, positions, ada_w, ada_b, norm_g, w_in, hg_lb_logits, hg_onorm_g, at_qnorm_g,
           at_knorm_g, s5_a_re, s5_a_im, s5_log_dt, s5_b_re, s5_b_im, s5_c_re, s5_c_im, s5_d,
           s5_glu_w, s5_glu_b, w_branch, w_out):
    B, S, D = x.shape
    L = ada_w.shape[0]
    lb_all = _lower_bounds(hg_lb_logits.astype(F32))
    mod = _modulation(c, ada_w, ada_b)
    w_all = _permute_w_in(w_in)
    tables = _rope_tables(positions)
    x2 = x.reshape(B * S, D)
    for l in range(L):
        shift = mod[l, :, None, 0:D]
        scale = mod[l, :, None, D:2 * D]
        gate = mod[l, :, None, 2 * D:3 * D]
        proj, hgf, u32 = _projection(x2, norm_g[l][None, :], shift, scale, w_all, l, S)
        ya = _hgrn(proj, hgf, lb_all[l][None, :], hg_onorm_g[l][None, :], B, S)
        yb = _dsa(proj, tables, at_qnorm_g[l][None, :], at_knorm_g[l][None, :], B, S)
        ops = _s5_prepare(s5_a_re[l], s5_a_im[l], s5_log_dt[l], s5_b_re[l], s5_b_im[l],
                          s5_c_re[l], s5_c_im[l])
        y5 = _s5_scan(u32, ops, B, S)
        x2 = _merge(x2, gate, ya, yb, y5, proj, s5_d[l][None, :], s5_glu_w[l].astype(BF16),
                    s5_glu_b[l][None, :], w_branch[l].astype(BF16), w_out[l].astype(BF16), S)
    return x2.reshape(B, S, D)
```
